```python
import jax, jax.numpy as jnp
from jax import lax
import numpy as np

D_MODEL = 1024
BATCH = 4
SEQ = 4096
DEPTH = 1
DEC_BATCH = 128
DEC_SEQ = 4
PAST_LEN = 16384
PAGE_SIZE = 128

CONV_W = D_MODEL // 2
CONV_K = 3
HEAD_DIM = 64
N_HEADS = (D_MODEL // 2) // HEAD_DIM
N_KV = 2
GROUP = N_HEADS // N_KV
ATT_W = N_HEADS * HEAD_DIM
KV_W = N_KV * HEAD_DIM
WINDOW = 128
Q_BLOCK = WINDOW
ATT_SCALE = HEAD_DIM ** -0.5
N_BRANCH = 2
IN_W = 3 * CONV_W + ATT_W + 2 * KV_W + N_BRANCH * D_MODEL
N_EXPERTS = 32
TOP_K = 4
D_FF = D_MODEL
SWIGLU_ALPHA = 1.702
SWIGLU_LIMIT = 7.0
MOE_BLOCK = 128
PLE_DIM = 256
RMS_EPS = 1e-5

kernel_name = "hybrid_gated_conv_swa_moe_step"


def _rmsnorm(x, g):
    xf = x.astype(jnp.float32)
    r = lax.rsqrt(jnp.mean(xf * xf, axis=-1, keepdims=True) + RMS_EPS)
    return (xf * r * g.astype(jnp.float32)).astype(x.dtype)


def _alibi_slopes():
    h = jnp.arange(1, N_HEADS + 1, dtype=jnp.float32)
    return jnp.exp2(-8.0 * h / N_HEADS).reshape(N_KV, GROUP)


def _short_conv(u, prev, conv_w):
    up = jnp.concatenate([prev, u], axis=1)
    T = u.shape[1]
    y = sum(conv_w[j] * up[:, j:j + T] for j in range(CONV_K))
    return y, up[:, -(CONV_K - 1):]


def _window_attend(q, k, v, q_pos, k_pos, sinks):
    s = jnp.einsum('bnqhgd,bnkhd->bnhgqk', q, k, preferred_element_type=jnp.float32) * ATT_SCALE
    dist = q_pos[:, :, None] - k_pos[:, None, :]
    valid = (dist >= 0) & (dist < WINDOW) & (k_pos[:, None, :] >= 0)
    s = s - _alibi_slopes()[:, :, None, None] * dist[:, None, None].astype(jnp.float32)
    s = jnp.where(valid[:, None, None], s, -jnp.inf)
    sink = sinks.astype(jnp.float32).reshape(N_KV, GROUP)[:, :, None, None]
    m = jnp.maximum(jnp.max(s, axis=-1, keepdims=True), sink)
    p = jnp.exp(s - m)
    denom = jnp.sum(p, axis=-1, keepdims=True) + jnp.exp(sink - m)
    o = jnp.einsum('bnhgqk,bnkhd->bnqhgd', p / denom, v.astype(jnp.float32))
    return o.astype(q.dtype)


def _attn_prompt(q, k, v, sinks, wb):
    B, S = q.shape[:2]
    nb = S // Q_BLOCK
    qb = q.reshape(B, nb, Q_BLOCK, N_KV, GROUP, HEAD_DIM)

    def band(t):
        tp = jnp.concatenate([jnp.zeros((B, WINDOW) + t.shape[2:], t.dtype), t], axis=1)
        tp = tp.reshape((B, nb + 1, Q_BLOCK) + t.shape[2:])
        return jnp.concatenate([tp[:, :-1], tp[:, 1:]], axis=2)

    q_pos = jnp.arange(S, dtype=jnp.int32).reshape(nb, Q_BLOCK)
    kp = jnp.arange(-WINDOW, S, dtype=jnp.int32).reshape(nb + 1, Q_BLOCK)
    k_pos = jnp.concatenate([kp[:-1], kp[1:]], axis=1)
    o = _window_attend(qb, band(k), band(v), q_pos, k_pos, sinks)
    return o.reshape(B, S, ATT_W), k[:, S - wb:], v[:, S - wb:]


def _attn_sample(q, k, v, k_prev, v_prev, sinks):
    N, T = q.shape[:2]
    wb = k_prev.shape[1]
    kk = jnp.concatenate([k_prev, k], axis=1)
    vv = jnp.concatenate([v_prev, v], axis=1)
    q_pos = PAST_LEN + jnp.arange(T, dtype=jnp.int32)
    k_pos = PAST_LEN - wb + jnp.arange(wb + T, dtype=jnp.int32)
    o = _window_attend(q[:, None], kk[:, None], vv[:, None], q_pos[None], k_pos[None], sinks)
    return o.reshape(N, T, ATT_W), kk[:, -wb:], vv[:, -wb:]


def _mixer(a, conv_prev, kv_prev, wb, w_in, conv_w, sinks, w_branch, w_out):
    N, T, _ = a.shape
    z = a @ w_in
    cuts = [CONV_W, 2 * CONV_W, 3 * CONV_W, 3 * CONV_W + ATT_W,
            3 * CONV_W + ATT_W + KV_W, 3 * CONV_W + ATT_W + 2 * KV_W]
    bg, cg, hc, q, k, v, gl = jnp.split(z, cuts, axis=-1)
    yc, conv_new = _short_conv(cg * hc, conv_prev, conv_w)
    yc = bg * yc
    q = q.reshape(N, T, N_KV, GROUP, HEAD_DIM)
    k = k.reshape(N, T, N_KV, HEAD_DIM)
    v = v.reshape(N, T, N_KV, HEAD_DIM)
    if kv_prev is None:
        ya, k_new, v_new = _attn_prompt(q, k, v, sinks, wb)
    else:
        ya, k_new, v_new = _attn_sample(q, k, v, kv_prev[0], kv_prev[1], sinks)
    yb = jnp.einsum('rntc,rcd->rntd', jnp.stack([yc, ya]), w_branch)
    gates = jax.nn.sigmoid(jnp.moveaxis(gl.reshape(N, T, N_BRANCH, D_MODEL), 2, 0).astype(jnp.float32))
    merged = jnp.sum(gates * yb.astype(jnp.float32), axis=0).astype(a.dtype)
    return merged @ w_out, conv_new, k_new, v_new


def _moe(x, w_router, b_router, w_up, b_up, w_down, b_down):
    T, D = x.shape
    logits = x.astype(jnp.float32) @ w_router.astype(jnp.float32) + b_router.astype(jnp.float32)
    top_v, top_i = lax.top_k(logits, TOP_K)
    gate = jax.nn.softmax(top_v, axis=-1)
    A = T * TOP_K
    flat_e = top_i.reshape(-1).astype(jnp.int32)
    order = jnp.argsort(flat_e)
    sorted_e = flat_e[order]
    sorted_tok = (order // TOP_K).astype(jnp.int32)
    counts = jnp.bincount(flat_e, length=N_EXPERTS).astype(jnp.int32)
    padded = (counts + MOE_BLOCK - 1) // MOE_BLOCK * MOE_BLOCK
    pad_end = jnp.cumsum(padded)
    pad_start = pad_end - padded
    start = jnp.cumsum(counts) - counts
    slot = pad_start[sorted_e] + (jnp.arange(A, dtype=jnp.int32) - start[sorted_e])
    n_blocks = -(-A // MOE_BLOCK) + N_EXPERTS
    P = n_blocks * MOE_BLOCK
    slot_tok = jnp.full((P,), T, jnp.int32).at[slot].set(sorted_tok)
    x_pad = jnp.concatenate([x, jnp.zeros((1, D), x.dtype)], axis=0)
    xs = x_pad[slot_tok].reshape(n_blocks, MOE_BLOCK, D)
    block_e = jnp.minimum(
        jnp.searchsorted(pad_end, jnp.arange(n_blocks, dtype=jnp.int32) * MOE_BLOCK, side='right'),
        N_EXPERTS - 1).astype(jnp.int32)

    def expert_block(args):
        xb, e = args
        h = (xb @ w_up[e] + b_up[e]).reshape(MOE_BLOCK, D_FF, 2)
        x_glu = jnp.minimum(h[..., 0], SWIGLU_LIMIT)
        x_lin = jnp.clip(h[..., 1], -SWIGLU_LIMIT, SWIGLU_LIMIT)
        act = x_glu * jax.nn.sigmoid(SWIGLU_ALPHA * x_glu) * (x_lin + 1)
        return act @ w_down[e] + b_down[e]

    ys = lax.map(expert_block, (xs, block_e)).reshape(P, D)
    slot_of = jnp.zeros((A,), jnp.int32).at[order].set(slot)
    y = ys[slot_of].reshape(T, TOP_K, D)
    return jnp.einsum('tk,tkd->td', gate.astype(y.dtype), y)


def _ple(h, p, g, w_gate, w_proj):
    gate = jax.nn.sigmoid((_rmsnorm(h, g) @ w_gate).astype(jnp.float32))
    return h + (gate * (p @ w_proj).astype(jnp.float32)).astype(h.dtype)


def setup_inputs(seed: int = 0) -> dict:
    key = jax.random.key(seed)
    ks = jax.random.split(key, 32)
    f32 = jnp.float32
    wb = min(WINDOW, PAST_LEN)
    nrm = lambda k, shape, s: jax.random.normal(k, shape, f32) * s
    gain = lambda k, shape: 1.0 + 0.01 * jax.random.normal(k, shape, f32)
    return {
        "x_prompt": nrm(ks[0], (BATCH, SEQ, D_MODEL), 1.0),
        "x_sample": nrm(ks[1], (DEC_BATCH, DEC_SEQ, D_MODEL), 1.0),
        "state_conv": nrm(ks[2], (DEPTH, DEC_BATCH, CONV_K - 1, CONV_W), 0.5),
        "cache_k": nrm(ks[3], (DEPTH, DEC_BATCH, wb, N_KV, HEAD_DIM), 1.0),
        "cache_v": nrm(ks[4], (DEPTH, DEC_BATCH, wb, N_KV, HEAD_DIM), 1.0),
        "p_prompt": nrm(ks[5], (DEPTH, BATCH, SEQ, PLE_DIM), 1.0),
        "p_sample": nrm(ks[6], (DEPTH, DEC_BATCH, DEC_SEQ, PLE_DIM), 1.0),
        "g_mix": gain(ks[7], (DEPTH, D_MODEL)),
        "w_in": nrm(ks[8], (DEPTH, D_MODEL, IN_W), D_MODEL ** -0.5),
        "conv_w": nrm(ks[9], (DEPTH, CONV_K, CONV_W), CONV_K ** -0.5),
        "sinks": nrm(ks[10], (DEPTH, N_HEADS), 0.5),
        "w_branch": nrm(ks[11], (DEPTH, N_BRANCH, CONV_W, D_MODEL), CONV_W ** -0.5),
        "w_out": nrm(ks[12], (DEPTH, D_MODEL, D_MODEL), D_MODEL ** -0.5),
        "g_ffn": gain(ks[13], (DEPTH, D_MODEL)),
        "w_router": nrm(ks[14], (DEPTH, D_MODEL, N_EXPERTS), D_MODEL ** -0.5),
        "b_router": nrm(ks[15], (DEPTH, N_EXPERTS), 0.01),
        "w_up": nrm(ks[16], (DEPTH, N_EXPERTS, D_MODEL, 2 * D_FF), D_MODEL ** -0.5),
        "b_up": nrm(ks[17], (DEPTH, N_EXPERTS, 2 * D_FF), 0.01),
        "w_down": nrm(ks[18], (DEPTH, N_EXPERTS, D_FF, D_MODEL), D_FF ** -0.5),
        "b_down": nrm(ks[19], (DEPTH, N_EXPERTS, D_MODEL), 0.01),
        "g_ple": gain(ks[20], (DEPTH, D_MODEL)),
        "w_ple_gate": nrm(ks[21], (DEPTH, D_MODEL, D_MODEL), D_MODEL ** -0.5),
        "w_ple": nrm(ks[22], (DEPTH, PLE_DIM, D_MODEL), PLE_DIM ** -0.5),
        "g_final": gain(ks[23], (D_MODEL,)),
    }


def reference(x_prompt, x_sample, state_conv, cache_k, cache_v, p_prompt, p_sample,
              g_mix, w_in, conv_w, sinks, w_branch, w_out, g_ffn, w_router, b_router,
              w_up, b_up, w_down, b_down, g_ple, w_ple_gate, w_ple, g_final):
    hp, hs = x_prompt, x_sample
    Bp, Sp, _ = hp.shape
    n_prompt = Bp * Sp
    wb = cache_k.shape[2]
    conv_p, conv_s, k_p, v_p, k_s, v_s = [], [], [], [], [], []
    for i in range(DEPTH):
        ap = _rmsnorm(hp, g_mix[i])
        as_ = _rmsnorm(hs, g_mix[i])
        zero_prev = jnp.zeros((Bp, CONV_K - 1, CONV_W), ap.dtype)
        op, cp, kp, vp = _mixer(ap, zero_prev, None, wb, w_in[i], conv_w[i], sinks[i], w_branch[i], w_out[i])
        os_, cs, ks_, vs_ = _mixer(as_, state_conv[i], (cache_k[i], cache_v[i]), wb,
                                   w_in[i], conv_w[i], sinks[i], w_branch[i], w_out[i])
        hp = hp + op
        hs = hs + os_
        m_all = jnp.concatenate([_rmsnorm(hp, g_ffn[i]).reshape(-1, D_MODEL),
                                 _rmsnorm(hs, g_ffn[i]).reshape(-1, D_MODEL)], axis=0)
        y_all = _moe(m_all, w_router[i], b_router[i], w_up[i], b_up[i], w_down[i], b_down[i])
        hp = hp + y_all[:n_prompt].reshape(hp.shape)
        hs = hs + y_all[n_prompt:].reshape(hs.shape)
        hp = _ple(hp, p_prompt[i], g_ple[i], w_ple_gate[i], w_ple[i])
        hs = _ple(hs, p_sample[i], g_ple[i], w_ple_gate[i], w_ple[i])
        conv_p.append(cp); conv_s.append(cs)
        k_p.append(kp); v_p.append(vp); k_s.append(ks_); v_s.append(vs_)
    y_prompt = _rmsnorm(hp, g_final)
    y_sample = _rmsnorm(hs, g_final)
    return (y_prompt, y_sample, jnp.stack(conv_p), jnp.stack(conv_s),
            jnp.stack(k_p), jnp.stack(v_p), jnp.stack(k_s), jnp.stack(v_s))
```

```python
import functools

import jax
import jax.numpy as jnp
from jax import lax
from jax.experimental import pallas as pl
from jax.experimental.pallas import tpu as pltpu

F32 = jnp.float32
BF16 = jnp.bfloat16
I32 = jnp.int32

D = 1024
CONV_W = 512
HEAD_DIM = 64
N_HEADS = 8
N_KV = 2
GROUP = N_HEADS // N_KV
ATT_W = N_HEADS * HEAD_DIM
KV_W = N_KV * HEAD_DIM
WINDOW = 128
ATT_SCALE = HEAD_DIM ** -0.5
N_EXPERTS = 32
TOP_K = 4
D_FF = 1024
PLE_DIM = 256
SWIGLU_ALPHA = 1.702
SWIGLU_LIMIT = 7.0
RMS_EPS = 1e-5
PAST_WB = 128

LANES = 128
SUBLANES = 8
VMEM_LIMIT = 56 * 1024 * 1024

TM = 512
TMT = 256
BLK = 256
SEQ_GROUP = 8

C_Q = 3 * CONV_W
C_KV = C_Q + ATT_W
C_GL = C_KV + 2 * KV_W
IN_W = C_GL + 2 * D
CS_KV = C_Q + N_HEADS * LANES
CS_GL = CS_KV + 2 * KV_W
IN_WS = CS_GL + 2 * D

NEG_INF = float("-inf")


def _slope(head):
    return 2.0 ** (-(head + 1))


def _rms(x, g):
    r = lax.rsqrt(jnp.mean(x * x, axis=-1, keepdims=True) + RMS_EPS)
    return x * r * g


def _div(x, n):
    assert n & (n - 1) == 0
    return lax.shift_right_logical(x, n.bit_length() - 1)


def _mod(x, n):
    assert n & (n - 1) == 0
    return x & (n - 1)


def _dot(a, b):
    return jnp.dot(a, b, preferred_element_type=F32)


def _dot_nt(a, b):
    return lax.dot_general(a, b, (((1,), (1,)), ((), ())), preferred_element_type=F32)


def _router(m, wr_ref, br_ref):
    logits = jnp.dot(m, wr_ref[...], preferred_element_type=F32,
                     precision=lax.Precision.HIGHEST) + br_ref[...]
    tm = logits.shape[0]
    lane = lax.broadcasted_iota(I32, (tm, LANES), 1)
    lanef = lane.astype(F32)
    vals, idxs = [], []
    l = logits
    for _ in range(TOP_K):
        mx = jnp.max(l, axis=-1, keepdims=True)
        idx = jnp.min(jnp.where(l == mx, lanef, float(LANES)), axis=-1, keepdims=True)
        vals.append(mx)
        idxs.append(idx.astype(I32))
        l = jnp.where(lanef == idx, NEG_INF, l)
    es = [jnp.exp(v - vals[0]) for v in vals]
    den = es[0] + es[1] + es[2] + es[3]
    topi = jnp.zeros((tm, LANES), I32)
    gate = jnp.zeros((tm, LANES), F32)
    for k in range(TOP_K):
        topi = jnp.where(lane == k, idxs[k], topi)
        gate = jnp.where(lane == k, es[k] / den, gate)
    return topi, gate


def _merge_out(x, a_bf, yc_bf, ya_bf, win_gl, w_br0, w_br1, wout_ref):
    zg = _dot(a_bf, win_gl)
    yb0 = _dot(yc_bf, w_br0)
    yb1 = _dot(ya_bf, w_br1)
    merged = jax.nn.sigmoid(zg[:, :D]) * yb0 + jax.nn.sigmoid(zg[:, D:]) * yb1
    return x + _dot(merged.astype(BF16), wout_ref[...])


def _mixer_prompt_kernel(x_ref, gmix_ref, win_ref, convw_ref, sinks_ref, wbr_ref, wout_ref,
                         gffn_ref, wr_ref, br_ref,
                         h1_ref, m_ref, topi_ref, gate_ref, conv_ref, kst_ref, vst_ref,
                         ucarry_ref, k4_ref, v4_ref, ya_ref, *, tiles_per_seq):
    ti = pl.program_id(0) % tiles_per_seq

    @pl.when(ti == 0)
    def _():
        ucarry_ref[...] = jnp.zeros_like(ucarry_ref)
        k4_ref[:, 0:WINDOW, :] = jnp.zeros((4, WINDOW, LANES), BF16)
        v4_ref[:, 0:WINDOW, :] = jnp.zeros((4, WINDOW, LANES), BF16)

    x = x_ref[...]
    a = _rms(x, gmix_ref[...]).astype(BF16)

    zc = _dot(a, win_ref[:, 0:C_Q])
    u = zc[:, CONV_W:2 * CONV_W] * zc[:, 2 * CONV_W:3 * CONV_W]
    row = lax.broadcasted_iota(I32, (TM, CONV_W), 0)
    c0 = ucarry_ref[SUBLANES - 2:SUBLANES - 1, :]
    c1 = ucarry_ref[SUBLANES - 1:SUBLANES, :]
    u1 = jnp.where(row == 0, c1, pltpu.roll(u, 1, axis=0))
    u2 = jnp.where(row == 0, c0, jnp.where(row == 1, c1, pltpu.roll(u, 2, axis=0)))
    cw = convw_ref[...]
    yc = zc[:, 0:CONV_W] * (cw[0:1] * u2 + cw[1:2] * u1 + cw[2:3] * u)
    ucarry_ref[...] = u[TM - SUBLANES:, :]
    conv_ref[0] = u[TM - SUBLANES:, :]

    zkv = _dot(a, win_ref[:, C_KV:C_GL])
    k = zkv[:, 0:KV_W]
    v = zkv[:, KV_W:]
    kst_ref[0] = k[TM - WINDOW:, :]
    vst_ref[0] = v[TM - WINDOW:, :]
    lo = lax.broadcasted_iota(I32, (TM, LANES), 1) < HEAD_DIM
    for src, dst in ((k, k4_ref), (v, v4_ref)):
        s_lo = jnp.where(lo, src, 0.0)
        s_hi = jnp.where(lo, 0.0, src)
        dst[0, WINDOW:, :] = s_lo.astype(BF16)
        dst[1, WINDOW:, :] = pltpu.roll(s_lo, HEAD_DIM, axis=1).astype(BF16)
        dst[2, WINDOW:, :] = pltpu.roll(s_hi, HEAD_DIM, axis=1).astype(BF16)
        dst[3, WINDOW:, :] = s_hi.astype(BF16)

    q = (_dot(a, win_ref[:, C_Q:C_KV]) * ATT_SCALE).astype(BF16)
    qi = lax.broadcasted_iota(I32, (WINDOW, 2 * WINDOW), 0)
    kj = lax.broadcasted_iota(I32, (WINDOW, 2 * WINDOW), 1)
    dist = qi - kj + WINDOW
    in_window = (dist >= 0) & (dist < WINDOW)
    distf = dist.astype(F32)
    lo_q = lax.broadcasted_iota(I32, (WINDOW, LANES), 1) < HEAD_DIM
    for b in range(TM // WINDOW):
        rb = b * WINDOW
        valid = in_window & ((kj + ti * TM) >= WINDOW) if b == 0 else in_window
        for pair in range(N_HEADS // 2):
            h2 = pair // (GROUP // 2)
            col = pair * LANES
            qp = q[rb:rb + WINDOW, col:col + LANES]
            acc = None
            inv = []
            for half in range(2):
                head = 2 * pair + half
                s = _dot_nt(qp, k4_ref[2 * h2 + half, rb:rb + 2 * WINDOW, :])
                s = jnp.where(valid, s - _slope(head) * distf, NEG_INF)
                sink = sinks_ref[head]
                mrow = jnp.maximum(jnp.max(s, axis=-1, keepdims=True), sink)
                p = jnp.exp(s - mrow)
                den = jnp.sum(p, axis=-1, keepdims=True) + jnp.exp(sink - mrow)
                o = _dot(p.astype(BF16), v4_ref[2 * h2 + half, rb:rb + 2 * WINDOW, :])
                acc = o if acc is None else acc + o
                inv.append(1.0 / den)
            ya_ref[rb:rb + WINDOW, col:col + LANES] = (
                acc * jnp.where(lo_q, inv[0], inv[1])).astype(BF16)
    k4_ref[:, 0:WINDOW, :] = k4_ref[:, TM:TM + WINDOW, :]
    v4_ref[:, 0:WINDOW, :] = v4_ref[:, TM:TM + WINDOW, :]

    h1 = _merge_out(x, a, yc.astype(BF16), ya_ref[...], win_ref[:, C_GL:IN_W],
                    wbr_ref[0], wbr_ref[1], wout_ref)
    h1_ref[...] = h1
    m = _rms(h1, gffn_ref[...])
    m_ref[...] = m
    topi, gate = _router(m, wr_ref, br_ref)
    topi_ref[...] = topi
    gate_ref[...] = gate


def _const_spec(shape):
    return pl.BlockSpec(shape, lambda *_: (0,) * len(shape))


def _mixer_prompt(xp, gmix, win, convw, sinks, wbr, wout, gffn, wr, br, n_total, seq):
    n_prompt = xp.shape[0]
    n_batch = n_prompt // seq
    tiles_per_seq = seq // TM
    grid = (n_prompt // TM,)
    row_spec = lambda w: pl.BlockSpec((TM, w), lambda i: (i, 0))
    state_spec = lambda r: pl.BlockSpec((1, r, LANES if r == WINDOW else CONV_W),
                                        lambda i: (i // tiles_per_seq, 0, 0))
    return pl.pallas_call(
        functools.partial(_mixer_prompt_kernel, tiles_per_seq=tiles_per_seq),
        grid=grid,
        in_specs=[
            row_spec(D), _const_spec((1, D)), _const_spec((D, IN_W)), _const_spec((3, CONV_W)),
            pl.BlockSpec(memory_space=pltpu.SMEM),
            _const_spec((2, CONV_W, D)), _const_spec((D, D)), _const_spec((1, D)),
            _const_spec((D, LANES)), _const_spec((1, LANES)),
        ],
        out_specs=[
            row_spec(D), row_spec(D), row_spec(LANES), row_spec(LANES),
            state_spec(SUBLANES), state_spec(WINDOW), state_spec(WINDOW),
        ],
        out_shape=[
            jax.ShapeDtypeStruct((n_total, D), F32), jax.ShapeDtypeStruct((n_total, D), F32),
            jax.ShapeDtypeStruct((n_total, LANES), I32), jax.ShapeDtypeStruct((n_total, LANES), F32),
            jax.ShapeDtypeStruct((n_batch, SUBLANES, CONV_W), F32),
            jax.ShapeDtypeStruct((n_batch, WINDOW, KV_W), F32),
            jax.ShapeDtypeStruct((n_batch, WINDOW, KV_W), F32),
        ],
        scratch_shapes=[
            pltpu.VMEM((SUBLANES, CONV_W), F32),
            pltpu.VMEM((4, TM + WINDOW, LANES), BF16),
            pltpu.VMEM((4, TM + WINDOW, LANES), BF16),
            pltpu.VMEM((TM, ATT_W), BF16),
        ],
        compiler_params=pltpu.CompilerParams(
            dimension_semantics=("arbitrary",), vmem_limit_bytes=VMEM_LIMIT),
        name="mixer_prompt",
    )(xp, gmix, win, convw, sinks, wbr, wout, gffn, wr, br)


def _mixer_sample_kernel(x_ref, gmix_ref, win_ref, convw_ref, st_ref, sinks_ref, ck_ref, cv_ref,
                         wbr0_ref, wbr1_ref, wout_ref, gffn_ref, wr_ref, br_ref,
                         h1_in, m_in, topi_in, gate_in,
                         h1_ref, m_ref, topi_ref, gate_ref, conv_ref, kout_ref, vout_ref,
                         a_ref, yc_ref, q_ref, knew_ref, vnew_ref, ya_ref, bias_ref,
                         *, n_seq, n_new):
    del h1_in, m_in, topi_in, gate_in
    g = pl.program_id(0)
    rows_c = SEQ_GROUP * PAST_WB
    rows_n = SEQ_GROUP * n_new
    q_rows = GROUP * rows_n

    @pl.when(g == 0)
    def _():
        x = x_ref[...]
        a = _rms(x, gmix_ref[...]).astype(BF16)
        a_ref[...] = a
        zc = _dot(a, win_ref[:, 0:C_Q])
        u = zc[:, CONV_W:2 * CONV_W] * zc[:, 2 * CONV_W:3 * CONV_W]
        up = [st_ref[0], st_ref[1]] + [u[t * n_seq:(t + 1) * n_seq] for t in range(n_new)]
        cw = convw_ref[...]
        for t in range(n_new):
            y = zc[t * n_seq:(t + 1) * n_seq, 0:CONV_W] * (
                cw[0:1] * up[t] + cw[1:2] * up[t + 1] + cw[2:3] * up[t + 2])
            yc_ref[t * n_seq:(t + 1) * n_seq, :] = y.astype(BF16)
        conv_ref[0] = up[n_new]
        conv_ref[1] = up[n_new + 1]
        q_ref[...] = _dot(a, win_ref[:, C_Q:CS_KV]) * ATT_SCALE
        zkv = _dot(a, win_ref[:, CS_KV:CS_GL])
        knew_ref[...] = zkv[:, 0:KV_W]
        vnew_ref[...] = zkv[:, KV_W:]
        r = lax.broadcasted_iota(I32, (q_rows, rows_c + rows_n), 0)
        c = lax.broadcasted_iota(I32, (q_rows, rows_c + rows_n), 1)
        hl = _div(r, rows_n)
        t = _div(_mod(r, rows_n), SEQ_GROUP)
        j = _mod(r, SEQ_GROUP)
        is_cache = c < rows_c
        cn = jnp.maximum(c - rows_c, 0)
        kseq = jnp.where(is_cache, _div(c, PAST_WB), _mod(cn, SEQ_GROUP))
        dist = jnp.where(is_cache, PAST_WB + t - _mod(c, PAST_WB), t - _div(cn, SEQ_GROUP))
        valid = (kseq == j) & (dist >= 0) & (dist < WINDOW)
        distf = dist.astype(F32)
        for h2 in range(N_KV):
            slope = jnp.zeros(r.shape, F32)
            for i in range(GROUP):
                slope = jnp.where(hl == i, _slope(h2 * GROUP + i), slope)
            bias_ref[h2] = jnp.where(valid, -(slope * distf), NEG_INF)

    base = pl.multiple_of(g * SEQ_GROUP, SEQ_GROUP)
    kc = ck_ref[...]
    vc = cv_ref[...]
    new_rows = lambda ref: [ref[pl.ds(t * n_seq + base, SEQ_GROUP), :] for t in range(n_new)]
    kgrp = jnp.concatenate([kc] + new_rows(knew_ref), axis=0).astype(BF16)
    vgrp = jnp.concatenate([vc] + new_rows(vnew_ref), axis=0).astype(BF16)
    hl_row = _div(lax.broadcasted_iota(I32, (q_rows, 1), 0), rows_n)
    for h2 in range(N_KV):
        qs = jnp.concatenate(
            [q_ref[pl.ds(t * n_seq + base, SEQ_GROUP), (h2 * GROUP + i) * LANES:(h2 * GROUP + i + 1) * LANES]
             for i in range(GROUP) for t in range(n_new)], axis=0).astype(BF16)
        s = _dot_nt(qs, kgrp) + bias_ref[h2]
        sink = jnp.zeros((q_rows, 1), F32)
        for i in range(GROUP):
            sink = jnp.where(hl_row == i, sinks_ref[h2 * GROUP + i], sink)
        mrow = jnp.maximum(jnp.max(s, axis=-1, keepdims=True), sink)
        p = jnp.exp(s - mrow)
        den = jnp.sum(p, axis=-1, keepdims=True) + jnp.exp(sink - mrow)
        o = _dot(p.astype(BF16), vgrp) * (1.0 / den)
        for i in range(GROUP):
            for t in range(n_new):
                r0 = (i * n_new + t) * SEQ_GROUP
                ya_ref[pl.ds(t * n_seq + base, SEQ_GROUP),
                       (h2 * GROUP + i) * LANES:(h2 * GROUP + i + 1) * LANES] = o[r0:r0 + SEQ_GROUP]

    for cache, new_ref, out in ((kc, knew_ref, kout_ref), (vc, vnew_ref, vout_ref)):
        out[...] = pltpu.roll(cache, rows_c - n_new, axis=0)
        for j in range(SEQ_GROUP):
            for t in range(n_new):
                out[pl.ds(j * PAST_WB + PAST_WB - n_new + t, 1), :] = (
                    new_ref[pl.ds(t * n_seq + base + j, 1), :])

    @pl.when(g == pl.num_programs(0) - 1)
    def _():
        h1 = _merge_out(x_ref[...], a_ref[...], yc_ref[...], ya_ref[...].astype(BF16),
                        win_ref[:, CS_GL:IN_WS], wbr0_ref[...], wbr1_ref[...], wout_ref)
        h1_ref[...] = h1
        m = _rms(h1, gffn_ref[...])
        m_ref[...] = m
        topi, gate = _router(m, wr_ref, br_ref)
        topi_ref[...] = topi
        gate_ref[...] = gate


def _mixer_sample(xs, gmix, win_s, convw, state, sinks, ck, cv, wbr0, wbr1e, wout, gffn, wr, br,
                  h1_all, m_all, topi_all, gate_all, n_seq, n_new):
    n_tok = xs.shape[0]
    n_total = h1_all.shape[0]
    blk0 = (n_total - n_tok) // n_tok
    rows_c = SEQ_GROUP * PAST_WB
    rows_n = SEQ_GROUP * n_new
    any_spec = pl.BlockSpec(memory_space=pl.ANY)
    tok_spec = lambda w: pl.BlockSpec((n_tok, w), lambda g: (blk0, 0))
    cache_spec = pl.BlockSpec((rows_c, KV_W), lambda g: (g, 0))
    return pl.pallas_call(
        functools.partial(_mixer_sample_kernel, n_seq=n_seq, n_new=n_new),
        grid=(n_seq // SEQ_GROUP,),
        in_specs=[
            _const_spec((n_tok, D)), _const_spec((1, D)), _const_spec((D, IN_WS)),
            _const_spec((3, CONV_W)), _const_spec((2, n_seq, CONV_W)),
            pl.BlockSpec(memory_space=pltpu.SMEM), cache_spec, cache_spec,
            _const_spec((CONV_W, D)), _const_spec((N_HEADS * LANES, D)), _const_spec((D, D)),
            _const_spec((1, D)), _const_spec((D, LANES)), _const_spec((1, LANES)),
            any_spec, any_spec, any_spec, any_spec,
        ],
        out_specs=[
            tok_spec(D), tok_spec(D), tok_spec(LANES), tok_spec(LANES),
            _const_spec((2, n_seq, CONV_W)), cache_spec, cache_spec,
        ],
        out_shape=[
            jax.ShapeDtypeStruct(h1_all.shape, F32), jax.ShapeDtypeStruct(m_all.shape, F32),
            jax.ShapeDtypeStruct(topi_all.shape, I32), jax.ShapeDtypeStruct(gate_all.shape, F32),
            jax.ShapeDtypeStruct((2, n_seq, CONV_W), F32),
            jax.ShapeDtypeStruct(ck.shape, F32), jax.ShapeDtypeStruct(cv.shape, F32),
        ],
        input_output_aliases={14: 0, 15: 1, 16: 2, 17: 3},
        scratch_shapes=[
            pltpu.VMEM((n_tok, D), BF16),
            pltpu.VMEM((n_tok, CONV_W), BF16),
            pltpu.VMEM((n_tok, N_HEADS * LANES), F32),
            pltpu.VMEM((n_tok, KV_W), F32), pltpu.VMEM((n_tok, KV_W), F32),
            pltpu.VMEM((n_tok, N_HEADS * LANES), F32),
            pltpu.VMEM((N_KV, GROUP * rows_n, rows_c + rows_n), F32),
        ],
        compiler_params=pltpu.CompilerParams(
            dimension_semantics=("arbitrary",), vmem_limit_bytes=VMEM_LIMIT),
        name="mixer_sample",
    )(xs, gmix, win_s, convw, state, sinks, ck, cv, wbr0, wbr1e, wout, gffn, wr, br,
      h1_all, m_all, topi_all, gate_all)


def _lane_cumsum(x):
    lane = lax.broadcasted_iota(I32, x.shape, 1)
    s = 1
    while s < LANES:
        x = x + jnp.where(lane >= s, pltpu.roll(x, s, axis=1), 0)
        s *= 2
    return x


def _route_kernel(topi_ref, slot_ref, tab_ref, meta_ref, cnt_ref, run_ref, pstart_ref, ltri_ref,
                  *, n_blocks_pad):
    p = pl.program_id(0)
    i = pl.program_id(1)
    lane = lax.broadcasted_iota(I32, (TM, LANES), 1)
    ti = topi_ref[...]
    e = [ti[:, k:k + 1] for k in range(TOP_K)]
    hot = jnp.zeros((TM, LANES), F32)
    for k in range(TOP_K):
        hot = hot + jnp.where(lane == e[k], 1.0, 0.0)
    colsum = jnp.sum(hot, axis=0, keepdims=True)

    @pl.when((p == 0) & (i == 0))
    def _():
        cnt_ref[...] = jnp.zeros_like(cnt_ref)
        r = lax.broadcasted_iota(I32, (TM, TM), 0)
        c = lax.broadcasted_iota(I32, (TM, TM), 1)
        ltri_ref[...] = jnp.where(r > c, 1.0, 0.0).astype(BF16)

    @pl.when(p == 0)
    def _():
        cnt_ref[...] += jnp.broadcast_to(colsum, cnt_ref.shape)

    @pl.when((p == 1) & (i == 0))
    def _():
        cnt = cnt_ref[...].astype(I32)
        nblk = _div(cnt + (BLK - 1), BLK)
        pend = _lane_cumsum(nblk)
        pstart = pend - nblk
        pstart_ref[...] = (pstart * BLK).astype(F32)
        run_ref[...] = jnp.zeros_like(run_ref)
        n_used = jnp.max(pend.astype(F32), axis=1, keepdims=True).astype(I32)
        brow = lax.broadcasted_iota(I32, (n_blocks_pad, LANES), 0)
        blane = lax.broadcasted_iota(I32, (n_blocks_pad, LANES), 1)
        done = jnp.where((pend[0:1] <= brow) & (blane < N_EXPERTS), 1.0, 0.0)
        block_e = jnp.minimum(jnp.sum(done, axis=1, keepdims=True).astype(I32), N_EXPERTS - 1)
        tab_ref[...] = jnp.broadcast_to(block_e, tab_ref.shape)
        mrow = lax.broadcasted_iota(I32, (SUBLANES, LANES), 0)
        meta = jnp.where(mrow == 0, cnt,
                         jnp.where(mrow == 1, pstart * BLK,
                                   jnp.where(mrow == 2, pend * BLK, n_used)))
        meta_ref[...] = meta

    @pl.when(p == 1)
    def _():
        before = _dot(ltri_ref[...], hot.astype(BF16))
        pos = pstart_ref[0:1] + run_ref[0:1] + before
        slot = jnp.zeros((TM, LANES), I32)
        for k in range(TOP_K):
            sk = jnp.sum(jnp.where(lane == e[k], pos, 0.0), axis=1, keepdims=True)
            slot = jnp.where(lane == k, sk.astype(I32), slot)
        slot_ref[...] = slot
        run_ref[...] += jnp.broadcast_to(colsum, run_ref.shape)


def _route(topi_all, n_blocks):
    n_total = topi_all.shape[0]
    n_blocks_pad = -(-n_blocks // SUBLANES) * SUBLANES
    stat = lambda: pltpu.VMEM((SUBLANES, LANES), F32)
    return pl.pallas_call(
        functools.partial(_route_kernel, n_blocks_pad=n_blocks_pad),
        grid=(2, n_total // TM),
        in_specs=[pl.BlockSpec((TM, LANES), lambda p, i: (i, 0))],
        out_specs=[
            pl.BlockSpec((TM, LANES), lambda p, i: (i * p, 0)),
            pl.BlockSpec((n_blocks_pad, LANES), lambda p, i: (0, 0)),
            pl.BlockSpec((SUBLANES, LANES), lambda p, i: (0, 0)),
        ],
        out_shape=[
            jax.ShapeDtypeStruct((n_total, LANES), I32),
            jax.ShapeDtypeStruct((n_blocks_pad, LANES), I32),
            jax.ShapeDtypeStruct((SUBLANES, LANES), I32),
        ],
        scratch_shapes=[stat(), stat(), stat(), pltpu.VMEM((TM, TM), BF16)],
        compiler_params=pltpu.CompilerParams(dimension_semantics=("arbitrary", "arbitrary")),
        name="route",
    )(topi_all)


def _pad_chunks():
    c = BLK // 2
    while c >= SUBLANES:
        yield c
        c //= 2


def _zero_fill(meta_ref, xs_ref, zbuf_ref, zsem, wait):
    def go(cp):
        cp.wait() if wait else cp.start()

    def body(e, carry):
        pos = meta_ref[N_EXPERTS + e] + meta_ref[e]
        aligned = (pos + (SUBLANES - 1)) & ~(SUBLANES - 1)
        for r in range(SUBLANES - 1):
            @pl.when(pos + r < aligned)
            def _():
                go(pltpu.make_async_copy(zbuf_ref.at[pl.ds(0, 1), :],
                                         xs_ref.at[pl.ds(pos + r, 1), :], zsem))
        n = meta_ref[2 * N_EXPERTS + e] - aligned
        for chunk in _pad_chunks():
            start = pl.multiple_of(aligned + (n & ~(2 * chunk - 1)), SUBLANES)

            @pl.when((n & chunk) != 0)
            def _():
                go(pltpu.make_async_copy(zbuf_ref.at[pl.ds(0, chunk), :],
                                         xs_ref.at[pl.ds(start, chunk), :], zsem))
        return carry
    lax.fori_loop(0, N_EXPERTS, body, 0)


def _dispatch_kernel(slot_ref, meta_ref, m_ref, xs_ref, zbuf_ref, sem, zsem):
    i = pl.program_id(0)
    base = i * (TM * TOP_K)

    def body(r, carry):
        for k in range(TOP_K):
            s = slot_ref[base + r * TOP_K + k]
            pltpu.make_async_copy(m_ref.at[pl.ds(r, 1), :], xs_ref.at[pl.ds(s, 1), :], sem).start()
        return carry
    lax.fori_loop(0, TM, body, 0, unroll=4)

    @pl.when(i == 0)
    def _():
        zbuf_ref[...] = jnp.zeros_like(zbuf_ref)
        _zero_fill(meta_ref, xs_ref, zbuf_ref, zsem, wait=False)

    for k in range(TOP_K):
        pltpu.make_async_copy(m_ref, xs_ref.at[pl.ds(0, TM), :], sem).wait()

    @pl.when(i == 0)
    def _():
        _zero_fill(meta_ref, xs_ref, zbuf_ref, zsem, wait=True)


def _dispatch(slot_flat, meta_flat, m_all, n_rows):
    n_total = m_all.shape[0]
    return pl.pallas_call(
        _dispatch_kernel,
        grid_spec=pltpu.PrefetchScalarGridSpec(
            num_scalar_prefetch=2,
            grid=(n_total // TM,),
            in_specs=[pl.BlockSpec((TM, D), lambda i, *_: (i, 0))],
            out_specs=pl.BlockSpec(memory_space=pl.ANY),
            scratch_shapes=[pltpu.VMEM((BLK // 2, D), F32),
                            pltpu.SemaphoreType.DMA, pltpu.SemaphoreType.DMA],
        ),
        out_shape=jax.ShapeDtypeStruct((n_rows, D), F32),
        compiler_params=pltpu.CompilerParams(dimension_semantics=("arbitrary",)),
        name="dispatch",
    )(slot_flat, meta_flat, m_all)


def _moe_kernel(be_ref, nu_ref, xs_ref, wup_ref, bup_ref, wdn_ref, bdn_ref, ys_ref):
    del be_ref

    @pl.when(pl.program_id(0) < nu_ref[0])
    def _():
        h = _dot(xs_ref[...].astype(BF16), wup_ref[0]) + bup_ref[0]
        glu = jnp.minimum(h[:, 0:D_FF], SWIGLU_LIMIT)
        lin = jnp.clip(h[:, D_FF:], -SWIGLU_LIMIT, SWIGLU_LIMIT)
        act = glu * jax.nn.sigmoid(SWIGLU_ALPHA * glu) * (lin + 1.0)
        ys_ref[...] = _dot(act.astype(BF16), wdn_ref[0]) + bdn_ref[0]


def _moe(block_e, n_used, xs, wup, bup, wdn, bdn):
    n_rows = xs.shape[0]
    blk = lambda b, be, nu: jnp.minimum(b, nu[0] - 1)
    expert = lambda b, be, nu: be[jnp.minimum(b, nu[0] - 1)]
    return pl.pallas_call(
        _moe_kernel,
        grid_spec=pltpu.PrefetchScalarGridSpec(
            num_scalar_prefetch=2,
            grid=(n_rows // BLK,),
            in_specs=[
                pl.BlockSpec((BLK, D), lambda b, be, nu: (blk(b, be, nu), 0)),
                pl.BlockSpec((1, D, 2 * D_FF), lambda b, be, nu: (expert(b, be, nu), 0, 0)),
                pl.BlockSpec((1, 1, 2 * D_FF), lambda b, be, nu: (expert(b, be, nu), 0, 0)),
                pl.BlockSpec((1, D_FF, D), lambda b, be, nu: (expert(b, be, nu), 0, 0)),
                pl.BlockSpec((1, 1, D), lambda b, be, nu: (expert(b, be, nu), 0, 0)),
            ],
            out_specs=pl.BlockSpec((BLK, D), lambda b, be, nu: (blk(b, be, nu), 0)),
        ),
        out_shape=jax.ShapeDtypeStruct((n_rows, D), F32),
        compiler_params=pltpu.CompilerParams(
            dimension_semantics=("arbitrary",), vmem_limit_bytes=VMEM_LIMIT),
        name="moe",
    )(block_e, n_used, xs, wup, bup, wdn, bdn)


def _tail_kernel(slot_ref, ys_ref, gate_ref, h1_ref, p_ref, gple_ref, wpg_ref, wple_ref, gfin_ref,
                 yp_ref, yd_ref, gbuf_ref, sem, *, prompt_tiles):
    i = pl.program_id(0)
    n = pl.num_programs(0)

    def issue(tile, buf):
        base = tile * (TMT * TOP_K)

        def body(r, carry):
            for k in range(TOP_K):
                s = slot_ref[base + r * TOP_K + k]
                pltpu.make_async_copy(ys_ref.at[pl.ds(s, 1), :],
                                      gbuf_ref.at[buf, k, pl.ds(r, 1), :], sem.at[buf]).start()
            return carry
        lax.fori_loop(0, TMT, body, 0, unroll=4)

    @pl.when(i == 0)
    def _():
        issue(0, 0)

    @pl.when(i + 1 < n)
    def _():
        issue(i + 1, (i + 1) % 2)

    buf = i % 2
    for k in range(TOP_K):
        pltpu.make_async_copy(ys_ref.at[pl.ds(0, TMT), :], gbuf_ref.at[buf, k], sem.at[buf]).wait()

    gate = gate_ref[...]
    moe = gate[:, 0:1] * gbuf_ref[buf, 0]
    for k in range(1, TOP_K):
        moe = moe + gate[:, k:k + 1] * gbuf_ref[buf, k]
    h2 = h1_ref[...] + moe
    pg = jax.nn.sigmoid(_dot(_rms(h2, gple_ref[...]).astype(BF16), wpg_ref[...]))
    h3 = h2 + pg * _dot(p_ref[...].astype(BF16), wple_ref[...])
    y = _rms(h3, gfin_ref[...])

    @pl.when(i < prompt_tiles)
    def _():
        yp_ref[...] = y

    @pl.when(i >= prompt_tiles)
    def _():
        yd_ref[...] = y


def _tail(slot_flat, ys, gate_all, h1_all, p_all, gple, wpg, wple, gfin, n_prompt):
    n_total = h1_all.shape[0]
    prompt_tiles = n_prompt // TMT
    row_spec = lambda w: pl.BlockSpec((TMT, w), lambda i, *_: (i, 0))
    return pl.pallas_call(
        functools.partial(_tail_kernel, prompt_tiles=prompt_tiles),
        grid_spec=pltpu.PrefetchScalarGridSpec(
            num_scalar_prefetch=1,
            grid=(n_total // TMT,),
            in_specs=[
                pl.BlockSpec(memory_space=pl.ANY),
                row_spec(LANES), row_spec(D), row_spec(PLE_DIM),
                pl.BlockSpec((1, D), lambda i, *_: (0, 0)),
                pl.BlockSpec((D, D), lambda i, *_: (0, 0)),
                pl.BlockSpec((PLE_DIM, D), lambda i, *_: (0, 0)),
                pl.BlockSpec((1, D), lambda i, *_: (0, 0)),
            ],
            out_specs=[
                pl.BlockSpec((TMT, D), lambda i, *_: (jnp.minimum(i, prompt_tiles - 1), 0)),
                pl.BlockSpec((TMT, D), lambda i, *_: (jnp.maximum(i - prompt_tiles, 0), 0)),
            ],
            scratch_shapes=[pltpu.VMEM((2, TOP_K, TMT, D), F32), pltpu.SemaphoreType.DMA((2,))],
        ),
        out_shape=[jax.ShapeDtypeStruct((n_prompt, D), F32),
                   jax.ShapeDtypeStruct((n_total - n_prompt, D), F32)],
        compiler_params=pltpu.CompilerParams(
            dimension_semantics=("arbitrary",), vmem_limit_bytes=VMEM_LIMIT),
        name="tail",
    )(slot_flat, ys, gate_all, h1_all, p_all, gple, wpg, wple, gfin)


def _widen_heads(w_q):
    wq = w_q.reshape(D, N_HEADS, HEAD_DIM)
    z = jnp.zeros_like(wq)
    lo = jnp.concatenate([wq, z], axis=-1)
    hi = jnp.concatenate([z, wq], axis=-1)
    kv_of_head = (jnp.arange(N_HEADS) // GROUP)[None, :, None]
    return jnp.where(kv_of_head == 0, lo, hi).reshape(D, N_HEADS * LANES)


def kernel(x_prompt, x_sample, state_conv, cache_k, cache_v, p_prompt, p_sample, g_mix, w_in, conv_w,
           sinks, w_branch, w_out, g_ffn, w_router, b_router, w_up, b_up, w_down, b_down, g_ple,
           w_ple_gate, w_ple, g_final):
    depth = w_in.shape[0]
    assert depth == 1, "single-layer step"
    n_batch, seq, _ = x_prompt.shape
    n_seq, n_new, _ = x_sample.shape
    wb = cache_k.shape[2]
    assert wb == PAST_WB == WINDOW and seq % TM == 0 and n_seq * n_new == TM
    n_prompt = n_batch * seq
    n_dec = n_seq * n_new
    n_total = n_prompt + n_dec
    n_blocks = -(-(n_total * TOP_K) // BLK) + N_EXPERTS
    n_rows = n_blocks * BLK

    win = w_in[0].astype(BF16)
    w_q = w_in[0][:, C_Q:C_KV]
    win_s = jnp.concatenate([w_in[0][:, :C_Q], _widen_heads(w_q), w_in[0][:, C_KV:]], axis=1).astype(BF16)
    wbr = w_branch[0].astype(BF16)
    wbr1e = jnp.transpose(_widen_heads(jnp.transpose(w_branch[0, 1]))).astype(BF16)
    wout = w_out[0].astype(BF16)
    wr = jnp.pad(w_router[0], ((0, 0), (0, LANES - N_EXPERTS)))
    br = jnp.pad(b_router[0], (0, LANES - N_EXPERTS), constant_values=-1e30).reshape(1, LANES)
    gmix = g_mix[0].reshape(1, D)
    gffn = g_ffn[0].reshape(1, D)
    wu = w_up[0].reshape(N_EXPERTS, D, D_FF, 2)
    wup = jnp.concatenate([wu[..., 0], wu[..., 1]], axis=-1).astype(BF16)
    bu = b_up[0].reshape(N_EXPERTS, D_FF, 2)
    bup = jnp.concatenate([bu[..., 0], bu[..., 1]], axis=-1).reshape(N_EXPERTS, 1, 2 * D_FF)
    wdn = w_down[0].astype(BF16)
    bdn = b_down[0].reshape(N_EXPERTS, 1, D)

    xp = x_prompt.reshape(n_prompt, D)
    h1_all, m_all, topi_all, gate_all, conv_p, k_p, v_p = _mixer_prompt(
        xp, gmix, win, conv_w[0], sinks[0], wbr, wout, gffn, wr, br, n_total, seq)
    xs_t = jnp.transpose(x_sample, (1, 0, 2)).reshape(n_dec, D)
    state_t = jnp.transpose(state_conv[0], (1, 0, 2))
    ck = cache_k[0].reshape(n_seq * wb, KV_W)
    cv = cache_v[0].reshape(n_seq * wb, KV_W)
    h1_all, m_all, topi_all, gate_all, conv_s, k_s, v_s = _mixer_sample(
        xs_t, gmix, win_s, conv_w[0], state_t, sinks[0], ck, cv, wbr[0], wbr1e, wout, gffn, wr, br,
        h1_all, m_all, topi_all, gate_all, n_seq, n_new)

    slot_w, tab, meta = _route(topi_all, n_blocks)
    slot_flat = slot_w[:, :TOP_K].reshape(-1)
    meta_flat = meta[0:3, :N_EXPERTS].reshape(-1)
    xs = _dispatch(slot_flat, meta_flat, m_all, n_rows)
    ys = _moe(tab[:n_blocks, 0], meta[3, 0:1], xs, wup, bup, wdn, bdn)

    p_all = jnp.concatenate([p_prompt[0].reshape(n_prompt, PLE_DIM),
                             jnp.transpose(p_sample[0], (1, 0, 2)).reshape(n_dec, PLE_DIM)], axis=0)
    y_p, y_d = _tail(slot_flat, ys, gate_all, h1_all, p_all, g_ple[0].reshape(1, D),
                     w_ple_gate[0].astype(BF16), w_ple[0].astype(BF16), g_final.reshape(1, D), n_prompt)

    y_prompt = y_p.reshape(n_batch, seq, D)
    y_sample = jnp.transpose(y_d.reshape(n_new, n_seq, D), (1, 0, 2))
    conv_prompt = conv_p[:, SUBLANES - 2:, :][None]
    conv_sample = jnp.transpose(conv_s, (1, 0, 2))[None]
    kv_shape = (1, -1, wb, N_KV, HEAD_DIM)
    return (y_prompt, y_sample, conv_prompt, conv_sample,
            k_p.reshape(kv_shape), v_p.reshape(kv_shape), k_s.reshape(kv_shape), v_s.reshape(kv_shape))
```

```python
import functools

import jax
import jax.numpy as jnp
from jax import lax
from jax.experimental import pallas as pl
from jax.experimental.pallas import tpu as pltpu

F32 = jnp.float32
BF16 = jnp.bfloat16
I32 = jnp.int32

D = 1024
CONV_W = 512
HEAD_DIM = 64
N_HEADS = 8
N_KV = 2
GROUP = N_HEADS // N_KV
ATT_W = N_HEADS * HEAD_DIM
KV_W = N_KV * HEAD_DIM
WINDOW = 128
ATT_SCALE = HEAD_DIM ** -0.5
N_EXPERTS = 32
TOP_K = 4
D_FF = 1024
PLE_DIM = 256
SWIGLU_ALPHA = 1.702
SWIGLU_LIMIT = 7.0
RMS_EPS = 1e-5
PAST_WB = 128

LANES = 128
SUBLANES = 8
VMEM_LIMIT = 56 * 1024 * 1024

TM = 512
TMT = 256
BLK = 256
SEQ_GROUP = 8

C_Q = 3 * CONV_W
C_KV = C_Q + ATT_W
C_GL = C_KV + 2 * KV_W
IN_W = C_GL + 2 * D
CS_KV = C_Q + N_HEADS * LANES
CS_GL = CS_KV + 2 * KV_W
IN_WS = CS_GL + 2 * D

NEG_INF = float("-inf")


def _slope(head):
    return 2.0 ** (-(head + 1))


def _rms(x, g):
    r = lax.rsqrt(jnp.mean(x * x, axis=-1, keepdims=True) + RMS_EPS)
    return x * r * g


def _div(x, n):
    assert n & (n - 1) == 0
    return lax.shift_right_logical(x, n.bit_length() - 1)


def _mod(x, n):
    assert n & (n - 1) == 0
    return x & (n - 1)


def _dot(a, b):
    return jnp.dot(a, b, preferred_element_type=F32)


def _dot_nt(a, b):
    return lax.dot_general(a, b, (((1,), (1,)), ((), ())), preferred_element_type=F32)


def _router(m, wr_ref, br_ref):
    logits = jnp.dot(m, wr_ref[...], preferred_element_type=F32,
                     precision=lax.Precision.HIGHEST) + br_ref[...]
    tm = logits.shape[0]
    lane = lax.broadcasted_iota(I32, (tm, LANES), 1)
    lanef = lane.astype(F32)
    vals, idxs = [], []
    l = logits
    for _ in range(TOP_K):
        mx = jnp.max(l, axis=-1, keepdims=True)
        idx = jnp.min(jnp.where(l == mx, lanef, float(LANES)), axis=-1, keepdims=True)
        vals.append(mx)
        idxs.append(idx.astype(I32))
        l = jnp.where(lanef == idx, NEG_INF, l)
    es = [jnp.exp(v - vals[0]) for v in vals]
    den = es[0] + es[1] + es[2] + es[3]
    topi = jnp.zeros((tm, LANES), I32)
    gate = jnp.zeros((tm, LANES), F32)
    for k in range(TOP_K):
        topi = jnp.where(lane == k, idxs[k], topi)
        gate = jnp.where(lane == k, es[k] / den, gate)
    return topi, gate


def _merge_out(x, a_bf, yc_bf, ya_bf, win_gl, w_br0, w_br1, wout_ref):
    zg = _dot(a_bf, win_gl)
    yb0 = _dot(yc_bf, w_br0)
    yb1 = _dot(ya_bf, w_br1)
    merged = jax.nn.sigmoid(zg[:, :D]) * yb0 + jax.nn.sigmoid(zg[:, D:]) * yb1
    return x + _dot(merged.astype(BF16), wout_ref[...])


def _mixer_prompt_kernel(*refs, tiles_per_seq, n_tiles):
    i = pl.program_id(0)

    @pl.when(i < n_tiles)
    def _():
        _mixer_prompt_tile(*refs, tiles_per_seq=tiles_per_seq)

    @pl.when(i == n_tiles)
    def _():
        for ref in refs[10:14]:
            ref[...] = jnp.zeros_like(ref)


def _mixer_prompt_tile(x_ref, gmix_ref, win_ref, convw_ref, sinks_ref, wbr_ref, wout_ref,
                       gffn_ref, wr_ref, br_ref,
                       h1_ref, m_ref, topi_ref, gate_ref, conv_ref, kst_ref, vst_ref,
                       ucarry_ref, k4_ref, v4_ref, ya_ref, *, tiles_per_seq):
    ti = pl.program_id(0) % tiles_per_seq

    @pl.when(ti == 0)
    def _():
        ucarry_ref[...] = jnp.zeros_like(ucarry_ref)
        k4_ref[:, 0:WINDOW, :] = jnp.zeros((4, WINDOW, LANES), BF16)
        v4_ref[:, 0:WINDOW, :] = jnp.zeros((4, WINDOW, LANES), BF16)

    x = x_ref[...]
    a = _rms(x, gmix_ref[...]).astype(BF16)

    zc = _dot(a, win_ref[:, 0:C_Q])
    u = zc[:, CONV_W:2 * CONV_W] * zc[:, 2 * CONV_W:3 * CONV_W]
    row = lax.broadcasted_iota(I32, (TM, CONV_W), 0)
    c0 = ucarry_ref[SUBLANES - 2:SUBLANES - 1, :]
    c1 = ucarry_ref[SUBLANES - 1:SUBLANES, :]
    u1 = jnp.where(row == 0, c1, pltpu.roll(u, 1, axis=0))
    u2 = jnp.where(row == 0, c0, jnp.where(row == 1, c1, pltpu.roll(u, 2, axis=0)))
    cw = convw_ref[...]
    yc = zc[:, 0:CONV_W] * (cw[0:1] * u2 + cw[1:2] * u1 + cw[2:3] * u)
    ucarry_ref[...] = u[TM - SUBLANES:, :]
    conv_ref[0] = u[TM - SUBLANES:, :]

    zkv = _dot(a, win_ref[:, C_KV:C_GL])
    k = zkv[:, 0:KV_W]
    v = zkv[:, KV_W:]
    kst_ref[0] = k[TM - WINDOW:, :]
    vst_ref[0] = v[TM - WINDOW:, :]
    lo = lax.broadcasted_iota(I32, (TM, LANES), 1) < HEAD_DIM
    for src, dst in ((k, k4_ref), (v, v4_ref)):
        s_lo = jnp.where(lo, src, 0.0)
        s_hi = jnp.where(lo, 0.0, src)
        dst[0, WINDOW:, :] = s_lo.astype(BF16)
        dst[1, WINDOW:, :] = pltpu.roll(s_lo, HEAD_DIM, axis=1).astype(BF16)
        dst[2, WINDOW:, :] = pltpu.roll(s_hi, HEAD_DIM, axis=1).astype(BF16)
        dst[3, WINDOW:, :] = s_hi.astype(BF16)

    q = (_dot(a, win_ref[:, C_Q:C_KV]) * ATT_SCALE).astype(BF16)
    qi = lax.broadcasted_iota(I32, (WINDOW, 2 * WINDOW), 0)
    kj = lax.broadcasted_iota(I32, (WINDOW, 2 * WINDOW), 1)
    dist = qi - kj + WINDOW
    in_window = (dist >= 0) & (dist < WINDOW)
    distf = dist.astype(F32)
    lo_q = lax.broadcasted_iota(I32, (WINDOW, LANES), 1) < HEAD_DIM
    for b in range(TM // WINDOW):
        rb = b * WINDOW
        valid = in_window & ((kj + ti * TM) >= WINDOW) if b == 0 else in_window
        for pair in range(N_HEADS // 2):
            h2 = pair // (GROUP // 2)
            col = pair * LANES
            qp = q[rb:rb + WINDOW, col:col + LANES]
            acc = None
            inv = []
            for half in range(2):
                head = 2 * pair + half
                s = _dot_nt(qp, k4_ref[2 * h2 + half, rb:rb + 2 * WINDOW, :])
                s = jnp.where(valid, s - _slope(head) * distf, NEG_INF)
                sink = sinks_ref[head]
                mrow = jnp.maximum(jnp.max(s, axis=-1, keepdims=True), sink)
                p = jnp.exp(s - mrow)
                den = jnp.sum(p, axis=-1, keepdims=True) + jnp.exp(sink - mrow)
                o = _dot(p.astype(BF16), v4_ref[2 * h2 + half, rb:rb + 2 * WINDOW, :])
                acc = o if acc is None else acc + o
                inv.append(1.0 / den)
            ya_ref[rb:rb + WINDOW, col:col + LANES] = (
                acc * jnp.where(lo_q, inv[0], inv[1])).astype(BF16)
    k4_ref[:, 0:WINDOW, :] = k4_ref[:, TM:TM + WINDOW, :]
    v4_ref[:, 0:WINDOW, :] = v4_ref[:, TM:TM + WINDOW, :]

    h1 = _merge_out(x, a, yc.astype(BF16), ya_ref[...], win_ref[:, C_GL:IN_W],
                    wbr_ref[0], wbr_ref[1], wout_ref)
    h1_ref[...] = h1
    m = _rms(h1, gffn_ref[...])
    m_ref[...] = m
    topi, gate = _router(m, wr_ref, br_ref)
    topi_ref[...] = topi
    gate_ref[...] = gate


def _const_spec(shape):
    return pl.BlockSpec(shape, lambda *_: (0,) * len(shape))


def _mixer_prompt(xp, gmix, win, convw, sinks, wbr, wout, gffn, wr, br, n_total, seq):
    n_prompt = xp.shape[0]
    n_batch = n_prompt // seq
    tiles_per_seq = seq // TM
    n_tiles = n_prompt // TM
    assert n_total == n_prompt + TM
    row_spec = lambda w: pl.BlockSpec((TM, w), lambda i: (i, 0))
    state_spec = lambda r: pl.BlockSpec(
        (1, r, LANES if r == WINDOW else CONV_W),
        lambda i: (jnp.minimum(i // tiles_per_seq, n_batch - 1), 0, 0))
    return pl.pallas_call(
        functools.partial(_mixer_prompt_kernel, tiles_per_seq=tiles_per_seq, n_tiles=n_tiles),
        grid=(n_tiles + 1,),
        in_specs=[
            pl.BlockSpec((TM, D), lambda i: (jnp.minimum(i, n_tiles - 1), 0)),
            _const_spec((1, D)), _const_spec((D, IN_W)), _const_spec((3, CONV_W)),
            pl.BlockSpec(memory_space=pltpu.SMEM),
            _const_spec((2, CONV_W, D)), _const_spec((D, D)), _const_spec((1, D)),
            _const_spec((D, LANES)), _const_spec((1, LANES)),
        ],
        out_specs=[
            row_spec(D), row_spec(D), row_spec(LANES), row_spec(LANES),
            state_spec(SUBLANES), state_spec(WINDOW), state_spec(WINDOW),
        ],
        out_shape=[
            jax.ShapeDtypeStruct((n_total, D), F32), jax.ShapeDtypeStruct((n_total, D), F32),
            jax.ShapeDtypeStruct((n_total, LANES), I32), jax.ShapeDtypeStruct((n_total, LANES), F32),
            jax.ShapeDtypeStruct((n_batch, SUBLANES, CONV_W), F32),
            jax.ShapeDtypeStruct((n_batch, WINDOW, KV_W), F32),
            jax.ShapeDtypeStruct((n_batch, WINDOW, KV_W), F32),
        ],
        scratch_shapes=[
            pltpu.VMEM((SUBLANES, CONV_W), F32),
            pltpu.VMEM((4, TM + WINDOW, LANES), BF16),
            pltpu.VMEM((4, TM + WINDOW, LANES), BF16),
            pltpu.VMEM((TM, ATT_W), BF16),
        ],
        compiler_params=pltpu.CompilerParams(
            dimension_semantics=("arbitrary",), vmem_limit_bytes=VMEM_LIMIT),
        name="mixer_prompt",
    )(xp, gmix, win, convw, sinks, wbr, wout, gffn, wr, br)


def _mixer_sample_kernel(x_ref, gmix_ref, win_ref, convw_ref, st_ref, sinks_ref, ck_ref, cv_ref,
                         wbr0_ref, wbr1_ref, wout_ref, gffn_ref, wr_ref, br_ref,
                         h1_in, m_in, topi_in, gate_in,
                         h1_ref, m_ref, topi_ref, gate_ref, conv_ref, kout_ref, vout_ref,
                         a_ref, yc_ref, q_ref, knew_ref, vnew_ref, ya_ref, bias_ref,
                         *, n_seq, n_new):
    del h1_in, m_in, topi_in, gate_in
    g = pl.program_id(0)
    rows_c = SEQ_GROUP * PAST_WB
    rows_n = SEQ_GROUP * n_new
    q_rows = GROUP * rows_n

    @pl.when(g == 0)
    def _():
        x = x_ref[...]
        a = _rms(x, gmix_ref[...]).astype(BF16)
        a_ref[...] = a
        zc = _dot(a, win_ref[:, 0:C_Q])
        u = zc[:, CONV_W:2 * CONV_W] * zc[:, 2 * CONV_W:3 * CONV_W]
        up = [st_ref[0], st_ref[1]] + [u[t * n_seq:(t + 1) * n_seq] for t in range(n_new)]
        cw = convw_ref[...]
        for t in range(n_new):
            y = zc[t * n_seq:(t + 1) * n_seq, 0:CONV_W] * (
                cw[0:1] * up[t] + cw[1:2] * up[t + 1] + cw[2:3] * up[t + 2])
            yc_ref[t * n_seq:(t + 1) * n_seq, :] = y.astype(BF16)
        conv_ref[0] = up[n_new]
        conv_ref[1] = up[n_new + 1]
        q_ref[...] = _dot(a, win_ref[:, C_Q:CS_KV]) * ATT_SCALE
        zkv = _dot(a, win_ref[:, CS_KV:CS_GL])
        knew_ref[...] = zkv[:, 0:KV_W]
        vnew_ref[...] = zkv[:, KV_W:]
        r = lax.broadcasted_iota(I32, (q_rows, rows_c + rows_n), 0)
        c = lax.broadcasted_iota(I32, (q_rows, rows_c + rows_n), 1)
        hl = _div(r, rows_n)
        t = _div(_mod(r, rows_n), SEQ_GROUP)
        j = _mod(r, SEQ_GROUP)
        is_cache = c < rows_c
        cn = jnp.maximum(c - rows_c, 0)
        kseq = jnp.where(is_cache, _div(c, PAST_WB), _mod(cn, SEQ_GROUP))
        dist = jnp.where(is_cache, PAST_WB + t - _mod(c, PAST_WB), t - _div(cn, SEQ_GROUP))
        valid = (kseq == j) & (dist >= 0) & (dist < WINDOW)
        distf = dist.astype(F32)
        for h2 in range(N_KV):
            slope = jnp.zeros(r.shape, F32)
            for i in range(GROUP):
                slope = jnp.where(hl == i, _slope(h2 * GROUP + i), slope)
            bias_ref[h2] = jnp.where(valid, -(slope * distf), NEG_INF)

    base = pl.multiple_of(g * SEQ_GROUP, SEQ_GROUP)
    kc = ck_ref[...]
    vc = cv_ref[...]
    new_rows = lambda ref: [ref[pl.ds(t * n_seq + base, SEQ_GROUP), :] for t in range(n_new)]
    kgrp = jnp.concatenate([kc] + new_rows(knew_ref), axis=0).astype(BF16)
    vgrp = jnp.concatenate([vc] + new_rows(vnew_ref), axis=0).astype(BF16)
    hl_row = _div(lax.broadcasted_iota(I32, (q_rows, 1), 0), rows_n)
    for h2 in range(N_KV):
        qs = jnp.concatenate(
            [q_ref[pl.ds(t * n_seq + base, SEQ_GROUP), (h2 * GROUP + i) * LANES:(h2 * GROUP + i + 1) * LANES]
             for i in range(GROUP) for t in range(n_new)], axis=0).astype(BF16)
        s = _dot_nt(qs, kgrp) + bias_ref[h2]
        sink = jnp.zeros((q_rows, 1), F32)
        for i in range(GROUP):
            sink = jnp.where(hl_row == i, sinks_ref[h2 * GROUP + i], sink)
        mrow = jnp.maximum(jnp.max(s, axis=-1, keepdims=True), sink)
        p = jnp.exp(s - mrow)
        den = jnp.sum(p, axis=-1, keepdims=True) + jnp.exp(sink - mrow)
        o = _dot(p.astype(BF16), vgrp) * (1.0 / den)
        for i in range(GROUP):
            for t in range(n_new):
                r0 = (i * n_new + t) * SEQ_GROUP
                ya_ref[pl.ds(t * n_seq + base, SEQ_GROUP),
                       (h2 * GROUP + i) * LANES:(h2 * GROUP + i + 1) * LANES] = o[r0:r0 + SEQ_GROUP]

    for cache, new_ref, out in ((kc, knew_ref, kout_ref), (vc, vnew_ref, vout_ref)):
        out[...] = pltpu.roll(cache, rows_c - n_new, axis=0)
        for j in range(SEQ_GROUP):
            for t in range(n_new):
                out[pl.ds(j * PAST_WB + PAST_WB - n_new + t, 1), :] = (
                    new_ref[pl.ds(t * n_seq + base + j, 1), :])

    @pl.when(g == pl.num_programs(0) - 1)
    def _():
        h1 = _merge_out(x_ref[...], a_ref[...], yc_ref[...], ya_ref[...].astype(BF16),
                        win_ref[:, CS_GL:IN_WS], wbr0_ref[...], wbr1_ref[...], wout_ref)
        h1_ref[...] = h1
        m = _rms(h1, gffn_ref[...])
        m_ref[...] = m
        topi, gate = _router(m, wr_ref, br_ref)
        topi_ref[...] = topi
        gate_ref[...] = gate


def _mixer_sample(xs, gmix, win_s, convw, state, sinks, ck, cv, wbr0, wbr1e, wout, gffn, wr, br,
                  h1_all, m_all, topi_all, gate_all, n_seq, n_new):
    n_tok = xs.shape[0]
    n_total = h1_all.shape[0]
    blk0 = (n_total - n_tok) // n_tok
    rows_c = SEQ_GROUP * PAST_WB
    rows_n = SEQ_GROUP * n_new
    any_spec = pl.BlockSpec(memory_space=pl.ANY)
    tok_spec = lambda w: pl.BlockSpec((n_tok, w), lambda g: (blk0, 0))
    cache_spec = pl.BlockSpec((rows_c, KV_W), lambda g: (g, 0))
    return pl.pallas_call(
        functools.partial(_mixer_sample_kernel, n_seq=n_seq, n_new=n_new),
        grid=(n_seq // SEQ_GROUP,),
        in_specs=[
            _const_spec((n_tok, D)), _const_spec((1, D)), _const_spec((D, IN_WS)),
            _const_spec((3, CONV_W)), _const_spec((2, n_seq, CONV_W)),
            pl.BlockSpec(memory_space=pltpu.SMEM), cache_spec, cache_spec,
            _const_spec((CONV_W, D)), _const_spec((N_HEADS * LANES, D)), _const_spec((D, D)),
            _const_spec((1, D)), _const_spec((D, LANES)), _const_spec((1, LANES)),
            any_spec, any_spec, any_spec, any_spec,
        ],
        out_specs=[
            tok_spec(D), tok_spec(D), tok_spec(LANES), tok_spec(LANES),
            _const_spec((2, n_seq, CONV_W)), cache_spec, cache_spec,
        ],
        out_shape=[
            jax.ShapeDtypeStruct(h1_all.shape, F32), jax.ShapeDtypeStruct(m_all.shape, F32),
            jax.ShapeDtypeStruct(topi_all.shape, I32), jax.ShapeDtypeStruct(gate_all.shape, F32),
            jax.ShapeDtypeStruct((2, n_seq, CONV_W), F32),
            jax.ShapeDtypeStruct(ck.shape, F32), jax.ShapeDtypeStruct(cv.shape, F32),
        ],
        input_output_aliases={14: 0, 15: 1, 16: 2, 17: 3},
        scratch_shapes=[
            pltpu.VMEM((n_tok, D), BF16),
            pltpu.VMEM((n_tok, CONV_W), BF16),
            pltpu.VMEM((n_tok, N_HEADS * LANES), F32),
            pltpu.VMEM((n_tok, KV_W), F32), pltpu.VMEM((n_tok, KV_W), F32),
            pltpu.VMEM((n_tok, N_HEADS * LANES), F32),
            pltpu.VMEM((N_KV, GROUP * rows_n, rows_c + rows_n), F32),
        ],
        compiler_params=pltpu.CompilerParams(
            dimension_semantics=("arbitrary",), vmem_limit_bytes=VMEM_LIMIT),
        name="mixer_sample",
    )(xs, gmix, win_s, convw, state, sinks, ck, cv, wbr0, wbr1e, wout, gffn, wr, br,
      h1_all, m_all, topi_all, gate_all)


def _lane_cumsum(x):
    lane = lax.broadcasted_iota(I32, x.shape, 1)
    s = 1
    while s < LANES:
        x = x + jnp.where(lane >= s, pltpu.roll(x, s, axis=1), 0)
        s *= 2
    return x


def _route_kernel(topi_ref, slot_ref, tab_ref, meta_ref, cnt_ref, run_ref, pstart_ref, ltri_ref,
                  *, n_blocks_pad):
    p = pl.program_id(0)
    i = pl.program_id(1)
    lane = lax.broadcasted_iota(I32, (TM, LANES), 1)
    ti = topi_ref[...]
    e = [ti[:, k:k + 1] for k in range(TOP_K)]
    hot = jnp.zeros((TM, LANES), F32)
    for k in range(TOP_K):
        hot = hot + jnp.where(lane == e[k], 1.0, 0.0)
    colsum = jnp.sum(hot, axis=0, keepdims=True)

    @pl.when((p == 0) & (i == 0))
    def _():
        cnt_ref[...] = jnp.zeros_like(cnt_ref)
        r = lax.broadcasted_iota(I32, (TM, TM), 0)
        c = lax.broadcasted_iota(I32, (TM, TM), 1)
        ltri_ref[...] = jnp.where(r > c, 1.0, 0.0).astype(BF16)

    @pl.when(p == 0)
    def _():
        cnt_ref[...] += jnp.broadcast_to(colsum, cnt_ref.shape)

    @pl.when((p == 1) & (i == 0))
    def _():
        cnt = cnt_ref[...].astype(I32)
        nblk = _div(cnt + (BLK - 1), BLK)
        pend = _lane_cumsum(nblk)
        pstart = pend - nblk
        pstart_ref[...] = (pstart * BLK).astype(F32)
        run_ref[...] = jnp.zeros_like(run_ref)
        n_used = jnp.max(pend.astype(F32), axis=1, keepdims=True).astype(I32)
        brow = lax.broadcasted_iota(I32, (n_blocks_pad, LANES), 0)
        blane = lax.broadcasted_iota(I32, (n_blocks_pad, LANES), 1)
        done = jnp.where((pend[0:1] <= brow) & (blane < N_EXPERTS), 1.0, 0.0)
        block_e = jnp.minimum(jnp.sum(done, axis=1, keepdims=True).astype(I32), N_EXPERTS - 1)
        tab_ref[...] = jnp.broadcast_to(block_e, tab_ref.shape)
        mrow = lax.broadcasted_iota(I32, (SUBLANES, LANES), 0)
        meta = jnp.where(mrow == 0, cnt,
                         jnp.where(mrow == 1, pstart * BLK,
                                   jnp.where(mrow == 2, pend * BLK, n_used)))
        meta_ref[...] = meta

    @pl.when(p == 1)
    def _():
        before = _dot(ltri_ref[...], hot.astype(BF16))
        pos = pstart_ref[0:1] + run_ref[0:1] + before
        slot = jnp.zeros((TM, LANES), I32)
        for k in range(TOP_K):
            sk = jnp.sum(jnp.where(lane == e[k], pos, 0.0), axis=1, keepdims=True)
            slot = jnp.where(lane == k, sk.astype(I32), slot)
        slot_ref[...] = slot
        run_ref[...] += jnp.broadcast_to(colsum, run_ref.shape)


def _route(topi_all, n_blocks):
    n_total = topi_all.shape[0]
    n_blocks_pad = -(-n_blocks // SUBLANES) * SUBLANES
    stat = lambda: pltpu.VMEM((SUBLANES, LANES), F32)
    return pl.pallas_call(
        functools.partial(_route_kernel, n_blocks_pad=n_blocks_pad),
        grid=(2, n_total // TM),
        in_specs=[pl.BlockSpec((TM, LANES), lambda p, i: (i, 0))],
        out_specs=[
            pl.BlockSpec((TM, LANES), lambda p, i: (i * p, 0)),
            pl.BlockSpec((n_blocks_pad, LANES), lambda p, i: (0, 0)),
            pl.BlockSpec((SUBLANES, LANES), lambda p, i: (0, 0)),
        ],
        out_shape=[
            jax.ShapeDtypeStruct((n_total, LANES), I32),
            jax.ShapeDtypeStruct((n_blocks_pad, LANES), I32),
            jax.ShapeDtypeStruct((SUBLANES, LANES), I32),
        ],
        scratch_shapes=[stat(), stat(), stat(), pltpu.VMEM((TM, TM), BF16)],
        compiler_params=pltpu.CompilerParams(dimension_semantics=("arbitrary", "arbitrary")),
        name="route",
    )(topi_all)


def _pad_chunks():
    c = BLK // 2
    while c >= SUBLANES:
        yield c
        c //= 2


def _zero_fill(meta_ref, xs_ref, zbuf_ref, zsem, wait):
    def go(cp):
        cp.wait() if wait else cp.start()

    def body(e, carry):
        pos = meta_ref[N_EXPERTS + e] + meta_ref[e]
        aligned = (pos + (SUBLANES - 1)) & ~(SUBLANES - 1)
        for r in range(SUBLANES - 1):
            @pl.when(pos + r < aligned)
            def _():
                go(pltpu.make_async_copy(zbuf_ref.at[pl.ds(0, 1), :],
                                         xs_ref.at[pl.ds(pos + r, 1), :], zsem))
        n = meta_ref[2 * N_EXPERTS + e] - aligned
        for chunk in _pad_chunks():
            start = pl.multiple_of(aligned + (n & ~(2 * chunk - 1)), SUBLANES)

            @pl.when((n & chunk) != 0)
            def _():
                go(pltpu.make_async_copy(zbuf_ref.at[pl.ds(0, chunk), :],
                                         xs_ref.at[pl.ds(start, chunk), :], zsem))
        return carry
    lax.fori_loop(0, N_EXPERTS, body, 0)

    def unused_block(blk, carry):
        for half in range(2):
            start = pl.multiple_of(blk * BLK + half * (BLK // 2), BLK // 2)
            go(pltpu.make_async_copy(zbuf_ref, xs_ref.at[pl.ds(start, BLK // 2), :], zsem))
        return carry
    lax.fori_loop(meta_ref[3 * N_EXPERTS], xs_ref.shape[0] // BLK, unused_block, 0)


def _dispatch_kernel(slot_ref, meta_ref, m_ref, xs_ref, zbuf_ref, sem, zsem):
    i = pl.program_id(0)
    base = i * (TM * TOP_K)

    def body(r, carry):
        for k in range(TOP_K):
            s = slot_ref[base + r * TOP_K + k]
            pltpu.make_async_copy(m_ref.at[pl.ds(r, 1), :], xs_ref.at[pl.ds(s, 1), :], sem).start()
        return carry
    lax.fori_loop(0, TM, body, 0, unroll=4)

    @pl.when(i == 0)
    def _():
        zbuf_ref[...] = jnp.zeros_like(zbuf_ref)
        _zero_fill(meta_ref, xs_ref, zbuf_ref, zsem, wait=False)

    for k in range(TOP_K):
        pltpu.make_async_copy(m_ref, xs_ref.at[pl.ds(0, TM), :], sem).wait()

    @pl.when(i == 0)
    def _():
        _zero_fill(meta_ref, xs_ref, zbuf_ref, zsem, wait=True)


def _dispatch(slot_flat, meta_flat, m_all, n_rows):
    n_total = m_all.shape[0]
    return pl.pallas_call(
        _dispatch_kernel,
        grid_spec=pltpu.PrefetchScalarGridSpec(
            num_scalar_prefetch=2,
            grid=(n_total // TM,),
            in_specs=[pl.BlockSpec((TM, D), lambda i, *_: (i, 0))],
            out_specs=pl.BlockSpec(memory_space=pl.ANY),
            scratch_shapes=[pltpu.VMEM((BLK // 2, D), F32),
                            pltpu.SemaphoreType.DMA, pltpu.SemaphoreType.DMA],
        ),
        out_shape=jax.ShapeDtypeStruct((n_rows, D), F32),
        compiler_params=pltpu.CompilerParams(dimension_semantics=("arbitrary",)),
        name="dispatch",
    )(slot_flat, meta_flat, m_all)


GLU_GROUP = 2 * LANES


def _moe_kernel(be_ref, nu_ref, xs_ref, wup_ref, bup_ref, wdn_ref, bdn_ref, ys_ref,
                wup_bf, wdn_bf, perm_ref):
    b = pl.program_id(0)

    @pl.when(b == 0)
    def _():
        r = lax.broadcasted_iota(I32, (GLU_GROUP, GLU_GROUP), 0)
        c = lax.broadcasted_iota(I32, (GLU_GROUP, GLU_GROUP), 1)
        src = jnp.where(c < LANES, 2 * c, 2 * (c - LANES) + 1)
        perm_ref[...] = jnp.where(r == src, 1.0, 0.0).astype(BF16)

    new_expert = (b == 0) | (be_ref[b] != be_ref[jnp.maximum(b - 1, 0)])

    @pl.when((b < nu_ref[0]) & new_expert)
    def _():
        for g in range(2 * D_FF // GLU_GROUP):
            wg = wup_ref[0, :, g * GLU_GROUP:(g + 1) * GLU_GROUP].astype(BF16)
            pg = _dot(wg, perm_ref[...]).astype(BF16)
            wup_bf[:, g * LANES:(g + 1) * LANES] = pg[:, 0:LANES]
            wup_bf[:, D_FF + g * LANES:D_FF + (g + 1) * LANES] = pg[:, LANES:]
        wdn_bf[...] = wdn_ref[0].astype(BF16)

    @pl.when(b < nu_ref[0])
    def _():
        h = _dot(xs_ref[...].astype(BF16), wup_bf[...]) + bup_ref[0]
        glu = jnp.minimum(h[:, 0:D_FF], SWIGLU_LIMIT)
        lin = jnp.clip(h[:, D_FF:], -SWIGLU_LIMIT, SWIGLU_LIMIT)
        act = glu * jax.nn.sigmoid(SWIGLU_ALPHA * glu) * (lin + 1.0)
        ys_ref[...] = _dot(act.astype(BF16), wdn_bf[...]) + bdn_ref[0]

    @pl.when(b >= nu_ref[0])
    def _():
        ys_ref[...] = jnp.zeros_like(ys_ref)


def _moe(block_e, n_used, xs, wup, bup, wdn, bdn):
    n_rows = xs.shape[0]
    blk = lambda b, be, nu: jnp.minimum(b, nu[0] - 1)
    expert = lambda b, be, nu: be[jnp.minimum(b, nu[0] - 1)]
    return pl.pallas_call(
        _moe_kernel,
        grid_spec=pltpu.PrefetchScalarGridSpec(
            num_scalar_prefetch=2,
            grid=(n_rows // BLK,),
            in_specs=[
                pl.BlockSpec((BLK, D), lambda b, be, nu: (blk(b, be, nu), 0)),
                pl.BlockSpec((1, D, 2 * D_FF), lambda b, be, nu: (expert(b, be, nu), 0, 0)),
                pl.BlockSpec((1, 1, 2 * D_FF), lambda b, be, nu: (expert(b, be, nu), 0, 0)),
                pl.BlockSpec((1, D_FF, D), lambda b, be, nu: (expert(b, be, nu), 0, 0)),
                pl.BlockSpec((1, 1, D), lambda b, be, nu: (expert(b, be, nu), 0, 0)),
            ],
            out_specs=pl.BlockSpec((BLK, D), lambda b, be, nu: (b, 0)),
            scratch_shapes=[pltpu.VMEM((D, 2 * D_FF), BF16), pltpu.VMEM((D_FF, D), BF16),
                            pltpu.VMEM((GLU_GROUP, GLU_GROUP), BF16)],
        ),
        out_shape=jax.ShapeDtypeStruct((n_rows, D), F32),
        compiler_params=pltpu.CompilerParams(
            dimension_semantics=("arbitrary",), vmem_limit_bytes=VMEM_LIMIT),
        name="moe",
    )(block_e, n_used, xs, wup, bup, wdn, bdn)


def _tail_kernel(slot_ref, ys_ref, gate_ref, h1_ref, pp_ref, pd_ref, gple_ref, wpg_ref, wple_ref,
                 gfin_ref, yp_ref, yd_ref, gbuf_ref, sem, *, prompt_tiles):
    i = pl.program_id(0)
    n = pl.num_programs(0)

    def issue(tile, buf):
        base = tile * (TMT * TOP_K)

        def body(r, carry):
            for k in range(TOP_K):
                s = slot_ref[base + r * TOP_K + k]
                pltpu.make_async_copy(ys_ref.at[pl.ds(s, 1), :],
                                      gbuf_ref.at[buf, k, pl.ds(r, 1), :], sem.at[buf]).start()
            return carry
        lax.fori_loop(0, TMT, body, 0, unroll=4)

    @pl.when(i == 0)
    def _():
        issue(0, 0)

    @pl.when(i + 1 < n)
    def _():
        issue(i + 1, (i + 1) % 2)

    buf = i % 2
    for k in range(TOP_K):
        pltpu.make_async_copy(ys_ref.at[pl.ds(0, TMT), :], gbuf_ref.at[buf, k], sem.at[buf]).wait()

    gate = gate_ref[...]
    moe = gate[:, 0:1] * gbuf_ref[buf, 0]
    for k in range(1, TOP_K):
        moe = moe + gate[:, k:k + 1] * gbuf_ref[buf, k]
    h2 = h1_ref[...] + moe
    pg = jax.nn.sigmoid(_dot(_rms(h2, gple_ref[...]).astype(BF16), wpg_ref[...]))
    p = jnp.where(i < prompt_tiles, pp_ref[...], pd_ref[...])
    h3 = h2 + pg * _dot(p.astype(BF16), wple_ref[...])
    y = _rms(h3, gfin_ref[...])

    @pl.when(i < prompt_tiles)
    def _():
        yp_ref[...] = y

    @pl.when(i >= prompt_tiles)
    def _():
        yd_ref[...] = y


def _tail(slot_flat, ys, gate_all, h1_all, p_prompt, p_dec, gple, wpg, wple, gfin):
    n_total = h1_all.shape[0]
    n_prompt = p_prompt.shape[0]
    prompt_tiles = n_prompt // TMT
    row_spec = lambda w: pl.BlockSpec((TMT, w), lambda i, *_: (i, 0))
    prompt_rows = lambda w: pl.BlockSpec((TMT, w), lambda i, *_: (jnp.minimum(i, prompt_tiles - 1), 0))
    dec_rows = lambda w: pl.BlockSpec((TMT, w), lambda i, *_: (jnp.maximum(i - prompt_tiles, 0), 0))
    return pl.pallas_call(
        functools.partial(_tail_kernel, prompt_tiles=prompt_tiles),
        grid_spec=pltpu.PrefetchScalarGridSpec(
            num_scalar_prefetch=1,
            grid=(n_total // TMT,),
            in_specs=[
                pl.BlockSpec(memory_space=pl.ANY),
                row_spec(LANES), row_spec(D), prompt_rows(PLE_DIM), dec_rows(PLE_DIM),
                pl.BlockSpec((1, D), lambda i, *_: (0, 0)),
                pl.BlockSpec((D, D), lambda i, *_: (0, 0)),
                pl.BlockSpec((PLE_DIM, D), lambda i, *_: (0, 0)),
                pl.BlockSpec((1, D), lambda i, *_: (0, 0)),
            ],
            out_specs=[prompt_rows(D), dec_rows(D)],
            scratch_shapes=[pltpu.VMEM((2, TOP_K, TMT, D), F32), pltpu.SemaphoreType.DMA((2,))],
        ),
        out_shape=[jax.ShapeDtypeStruct((n_prompt, D), F32),
                   jax.ShapeDtypeStruct((n_total - n_prompt, D), F32)],
        compiler_params=pltpu.CompilerParams(
            dimension_semantics=("arbitrary",), vmem_limit_bytes=VMEM_LIMIT),
        name="tail",
    )(slot_flat, ys, gate_all, h1_all, p_prompt, p_dec, gple, wpg, wple, gfin)


def _widen_heads(w_q):
    wq = w_q.reshape(D, N_HEADS, HEAD_DIM)
    z = jnp.zeros_like(wq)
    lo = jnp.concatenate([wq, z], axis=-1)
    hi = jnp.concatenate([z, wq], axis=-1)
    kv_of_head = (jnp.arange(N_HEADS) // GROUP)[None, :, None]
    return jnp.where(kv_of_head == 0, lo, hi).reshape(D, N_HEADS * LANES)


def kernel(x_prompt, x_sample, state_conv, cache_k, cache_v, p_prompt, p_sample, g_mix, w_in, conv_w,
           sinks, w_branch, w_out, g_ffn, w_router, b_router, w_up, b_up, w_down, b_down, g_ple,
           w_ple_gate, w_ple, g_final):
    depth = w_in.shape[0]
    assert depth == 1, "single-layer step"
    n_batch, seq, _ = x_prompt.shape
    n_seq, n_new, _ = x_sample.shape
    wb = cache_k.shape[2]
    assert wb == PAST_WB == WINDOW and seq % TM == 0 and n_seq * n_new == TM
    n_prompt = n_batch * seq
    n_dec = n_seq * n_new
    n_total = n_prompt + n_dec
    n_blocks = -(-(n_total * TOP_K) // BLK) + N_EXPERTS
    n_rows = n_blocks * BLK

    win = w_in[0].astype(BF16)
    w_q = w_in[0][:, C_Q:C_KV]
    win_s = jnp.concatenate([w_in[0][:, :C_Q], _widen_heads(w_q), w_in[0][:, C_KV:]], axis=1).astype(BF16)
    wbr = w_branch[0].astype(BF16)
    wbr1e = jnp.transpose(_widen_heads(jnp.transpose(w_branch[0, 1]))).astype(BF16)
    wout = w_out[0].astype(BF16)
    wr = jnp.pad(w_router[0], ((0, 0), (0, LANES - N_EXPERTS)))
    br = jnp.pad(b_router[0], (0, LANES - N_EXPERTS), constant_values=-1e30).reshape(1, LANES)
    gmix = g_mix[0].reshape(1, D)
    gffn = g_ffn[0].reshape(1, D)
    bu = b_up[0].reshape(N_EXPERTS, D_FF, 2)
    bup = jnp.concatenate([bu[..., 0], bu[..., 1]], axis=-1).reshape(N_EXPERTS, 1, 2 * D_FF)
    bdn = b_down[0].reshape(N_EXPERTS, 1, D)

    xp = x_prompt.reshape(n_prompt, D)
    h1_all, m_all, topi_all, gate_all, conv_p, k_p, v_p = _mixer_prompt(
        xp, gmix, win, conv_w[0], sinks[0], wbr, wout, gffn, wr, br, n_total, seq)
    xs_t = jnp.transpose(x_sample, (1, 0, 2)).reshape(n_dec, D)
    state_t = jnp.transpose(state_conv[0], (1, 0, 2))
    ck = cache_k[0].reshape(n_seq * wb, KV_W)
    cv = cache_v[0].reshape(n_seq * wb, KV_W)
    h1_all, m_all, topi_all, gate_all, conv_s, k_s, v_s = _mixer_sample(
        xs_t, gmix, win_s, conv_w[0], state_t, sinks[0], ck, cv, wbr[0], wbr1e, wout, gffn, wr, br,
        h1_all, m_all, topi_all, gate_all, n_seq, n_new)

    slot_w, tab, meta = _route(topi_all, n_blocks)
    slot_flat = slot_w[:, :TOP_K].reshape(-1)
    n_used = meta[3, 0:1]
    meta_flat = jnp.concatenate([meta[0:3, :N_EXPERTS].reshape(-1), n_used])
    xs = _dispatch(slot_flat, meta_flat, m_all, n_rows)
    ys = _moe(tab[:n_blocks, 0], n_used, xs, w_up[0], bup, w_down[0], bdn)

    y_p, y_d = _tail(slot_flat, ys, gate_all, h1_all, p_prompt[0].reshape(n_prompt, PLE_DIM),
                     jnp.transpose(p_sample[0], (1, 0, 2)).reshape(n_dec, PLE_DIM),
                     g_ple[0].reshape(1, D), w_ple_gate[0].astype(BF16), w_ple[0].astype(BF16),
                     g_final.reshape(1, D))

    y_prompt = y_p.reshape(n_batch, seq, D)
    y_sample = jnp.transpose(y_d.reshape(n_new, n_seq, D), (1, 0, 2))
    conv_prompt = conv_p[:, SUBLANES - 2:, :][None]
    conv_sample = jnp.transpose(conv_s, (1, 0, 2))[None]
    kv_shape = (1, -1, wb, N_KV, HEAD_DIM)
    return (y_prompt, y_sample, conv_prompt, conv_sample,
            k_p.reshape(kv_shape), v_p.reshape(kv_shape), k_s.reshape(kv_shape), v_s.reshape(kv_shape))
```

```python
import functools

import jax
import jax.numpy as jnp
from jax import lax
from jax.experimental import pallas as pl
from jax.experimental.pallas import tpu as pltpu

F32 = jnp.float32
BF16 = jnp.bfloat16
I32 = jnp.int32

D = 1024
CONV_W = 512
HEAD_DIM = 64
N_HEADS = 8
N_KV = 2
GROUP = N_HEADS // N_KV
ATT_W = N_HEADS * HEAD_DIM
KV_W = N_KV * HEAD_DIM
WINDOW = 128
ATT_SCALE = HEAD_DIM ** -0.5
N_EXPERTS = 32
TOP_K = 4
D_FF = 1024
PLE_DIM = 256
SWIGLU_ALPHA = 1.702
SWIGLU_LIMIT = 7.0
RMS_EPS = 1e-5
PAST_WB = 128

LANES = 128
SUBLANES = 8
VMEM_LIMIT = 56 * 1024 * 1024

TM = 512
TMT = 256
BLK = 256
SEQ_GROUP = 8

C_Q = 3 * CONV_W
C_KV = C_Q + ATT_W
C_GL = C_KV + 2 * KV_W
IN_W = C_GL + 2 * D
CS_KV = C_Q + N_HEADS * LANES
CS_GL = CS_KV + 2 * KV_W
IN_WS = CS_GL + 2 * D

NEG_INF = float("-inf")


def _slope(head):
    return 2.0 ** (-(head + 1))


def _rms(x, g):
    r = lax.rsqrt(jnp.mean(x * x, axis=-1, keepdims=True) + RMS_EPS)
    return x * r * g


def _div(x, n):
    assert n & (n - 1) == 0
    return lax.shift_right_logical(x, n.bit_length() - 1)


def _mod(x, n):
    assert n & (n - 1) == 0
    return x & (n - 1)


def _dot(a, b):
    return jnp.dot(a, b, preferred_element_type=F32)


def _dot_nt(a, b):
    return lax.dot_general(a, b, (((1,), (1,)), ((), ())), preferred_element_type=F32)


def _router(m, wr_ref, br_ref):
    logits = jnp.dot(m, wr_ref[...], preferred_element_type=F32,
                     precision=lax.Precision.HIGHEST) + br_ref[...]
    tm = logits.shape[0]
    lane = lax.broadcasted_iota(I32, (tm, LANES), 1)
    lanef = lane.astype(F32)
    vals, idxs = [], []
    l = logits
    for _ in range(TOP_K):
        mx = jnp.max(l, axis=-1, keepdims=True)
        idx = jnp.min(jnp.where(l == mx, lanef, float(LANES)), axis=-1, keepdims=True)
        vals.append(mx)
        idxs.append(idx.astype(I32))
        l = jnp.where(lanef == idx, NEG_INF, l)
    es = [jnp.exp(v - vals[0]) for v in vals]
    den = es[0] + es[1] + es[2] + es[3]
    topi = jnp.zeros((tm, LANES), I32)
    gate = jnp.zeros((tm, LANES), F32)
    for k in range(TOP_K):
        topi = jnp.where(lane == k, idxs[k], topi)
        gate = jnp.where(lane == k, es[k] / den, gate)
    return topi, gate


def _merge_out(x, a_bf, yc_bf, ya_bf, win_gl, w_br0, w_br1, wout_ref):
    zg = _dot(a_bf, win_gl)
    yb0 = _dot(yc_bf, w_br0)
    yb1 = _dot(ya_bf, w_br1)
    merged = jax.nn.sigmoid(zg[:, :D]) * yb0 + jax.nn.sigmoid(zg[:, D:]) * yb1
    return x + _dot(merged.astype(BF16), wout_ref[...])


def _mixer_prompt_kernel(*refs, tiles_per_seq, n_tiles):
    i = pl.program_id(0)

    @pl.when(i < n_tiles)
    def _():
        _mixer_prompt_tile(*refs, tiles_per_seq=tiles_per_seq)

    @pl.when(i == n_tiles)
    def _():
        for ref in refs[10:14]:
            ref[...] = jnp.zeros_like(ref)


def _mixer_prompt_tile(x_ref, gmix_ref, win_ref, convw_ref, sinks_ref, wbr_ref, wout_ref,
                       gffn_ref, wr_ref, br_ref,
                       h1_ref, m_ref, topi_ref, gate_ref, conv_ref, kst_ref, vst_ref,
                       ucarry_ref, k4_ref, v4_ref, ya_ref, *, tiles_per_seq):
    ti = pl.program_id(0) % tiles_per_seq

    @pl.when(ti == 0)
    def _():
        ucarry_ref[...] = jnp.zeros_like(ucarry_ref)
        k4_ref[:, 0:WINDOW, :] = jnp.zeros((4, WINDOW, LANES), BF16)
        v4_ref[:, 0:WINDOW, :] = jnp.zeros((4, WINDOW, LANES), BF16)

    x = x_ref[...]
    a = _rms(x, gmix_ref[...]).astype(BF16)

    zc = _dot(a, win_ref[:, 0:C_Q])
    u = zc[:, CONV_W:2 * CONV_W] * zc[:, 2 * CONV_W:3 * CONV_W]
    row = lax.broadcasted_iota(I32, (TM, CONV_W), 0)
    c0 = ucarry_ref[SUBLANES - 2:SUBLANES - 1, :]
    c1 = ucarry_ref[SUBLANES - 1:SUBLANES, :]
    u1 = jnp.where(row == 0, c1, pltpu.roll(u, 1, axis=0))
    u2 = jnp.where(row == 0, c0, jnp.where(row == 1, c1, pltpu.roll(u, 2, axis=0)))
    cw = convw_ref[...]
    yc = zc[:, 0:CONV_W] * (cw[0:1] * u2 + cw[1:2] * u1 + cw[2:3] * u)
    ucarry_ref[...] = u[TM - SUBLANES:, :]
    conv_ref[0] = u[TM - SUBLANES:, :]

    zkv = _dot(a, win_ref[:, C_KV:C_GL])
    k = zkv[:, 0:KV_W]
    v = zkv[:, KV_W:]
    kst_ref[0] = k[TM - WINDOW:, :]
    vst_ref[0] = v[TM - WINDOW:, :]
    lo = lax.broadcasted_iota(I32, (TM, LANES), 1) < HEAD_DIM
    for src, dst in ((k, k4_ref), (v, v4_ref)):
        s_lo = jnp.where(lo, src, 0.0)
        s_hi = jnp.where(lo, 0.0, src)
        dst[0, WINDOW:, :] = s_lo.astype(BF16)
        dst[1, WINDOW:, :] = pltpu.roll(s_lo, HEAD_DIM, axis=1).astype(BF16)
        dst[2, WINDOW:, :] = pltpu.roll(s_hi, HEAD_DIM, axis=1).astype(BF16)
        dst[3, WINDOW:, :] = s_hi.astype(BF16)

    q = (_dot(a, win_ref[:, C_Q:C_KV]) * ATT_SCALE).astype(BF16)
    qi = lax.broadcasted_iota(I32, (WINDOW, 2 * WINDOW), 0)
    kj = lax.broadcasted_iota(I32, (WINDOW, 2 * WINDOW), 1)
    dist = qi - kj + WINDOW
    in_window = (dist >= 0) & (dist < WINDOW)
    distf = dist.astype(F32)
    lo_q = lax.broadcasted_iota(I32, (WINDOW, LANES), 1) < HEAD_DIM
    for b in range(TM // WINDOW):
        rb = b * WINDOW
        valid = in_window & ((kj + ti * TM) >= WINDOW) if b == 0 else in_window
        for pair in range(N_HEADS // 2):
            h2 = pair // (GROUP // 2)
            col = pair * LANES
            qp = q[rb:rb + WINDOW, col:col + LANES]
            acc = None
            inv = []
            for half in range(2):
                head = 2 * pair + half
                s = _dot_nt(qp, k4_ref[2 * h2 + half, rb:rb + 2 * WINDOW, :])
                s = jnp.where(valid, s - _slope(head) * distf, NEG_INF)
                sink = sinks_ref[head]
                mrow = jnp.maximum(jnp.max(s, axis=-1, keepdims=True), sink)
                p = jnp.exp(s - mrow)
                den = jnp.sum(p, axis=-1, keepdims=True) + jnp.exp(sink - mrow)
                o = _dot(p.astype(BF16), v4_ref[2 * h2 + half, rb:rb + 2 * WINDOW, :])
                acc = o if acc is None else acc + o
                inv.append(1.0 / den)
            ya_ref[rb:rb + WINDOW, col:col + LANES] = (
                acc * jnp.where(lo_q, inv[0], inv[1])).astype(BF16)
    k4_ref[:, 0:WINDOW, :] = k4_ref[:, TM:TM + WINDOW, :]
    v4_ref[:, 0:WINDOW, :] = v4_ref[:, TM:TM + WINDOW, :]

    h1 = _merge_out(x, a, yc.astype(BF16), ya_ref[...], win_ref[:, C_GL:IN_W],
                    wbr_ref[0], wbr_ref[1], wout_ref)
    h1_ref[...] = h1
    m = _rms(h1, gffn_ref[...])
    m_ref[...] = m
    topi, gate = _router(m, wr_ref, br_ref)
    topi_ref[...] = topi
    gate_ref[...] = gate


def _const_spec(shape):
    return pl.BlockSpec(shape, lambda *_: (0,) * len(shape))


def _mixer_prompt(xp, gmix, win, convw, sinks, wbr, wout, gffn, wr, br, n_total, seq):
    n_prompt = xp.shape[0]
    n_batch = n_prompt // seq
    tiles_per_seq = seq // TM
    n_tiles = n_prompt // TM
    assert n_total == n_prompt + TM
    row_spec = lambda w: pl.BlockSpec((TM, w), lambda i: (i, 0))
    state_spec = lambda r: pl.BlockSpec(
        (1, r, LANES if r == WINDOW else CONV_W),
        lambda i: (jnp.minimum(i // tiles_per_seq, n_batch - 1), 0, 0))
    return pl.pallas_call(
        functools.partial(_mixer_prompt_kernel, tiles_per_seq=tiles_per_seq, n_tiles=n_tiles),
        grid=(n_tiles + 1,),
        in_specs=[
            pl.BlockSpec((TM, D), lambda i: (jnp.minimum(i, n_tiles - 1), 0)),
            _const_spec((1, D)), _const_spec((D, IN_W)), _const_spec((3, CONV_W)),
            pl.BlockSpec(memory_space=pltpu.SMEM),
            _const_spec((2, CONV_W, D)), _const_spec((D, D)), _const_spec((1, D)),
            _const_spec((D, LANES)), _const_spec((1, LANES)),
        ],
        out_specs=[
            row_spec(D), row_spec(D), row_spec(LANES), row_spec(LANES),
            state_spec(SUBLANES), state_spec(WINDOW), state_spec(WINDOW),
        ],
        out_shape=[
            jax.ShapeDtypeStruct((n_total, D), F32), jax.ShapeDtypeStruct((n_total, D), F32),
            jax.ShapeDtypeStruct((n_total, LANES), I32), jax.ShapeDtypeStruct((n_total, LANES), F32),
            jax.ShapeDtypeStruct((n_batch, SUBLANES, CONV_W), F32),
            jax.ShapeDtypeStruct((n_batch, WINDOW, KV_W), F32),
            jax.ShapeDtypeStruct((n_batch, WINDOW, KV_W), F32),
        ],
        scratch_shapes=[
            pltpu.VMEM((SUBLANES, CONV_W), F32),
            pltpu.VMEM((4, TM + WINDOW, LANES), BF16),
            pltpu.VMEM((4, TM + WINDOW, LANES), BF16),
            pltpu.VMEM((TM, ATT_W), BF16),
        ],
        compiler_params=pltpu.CompilerParams(
            dimension_semantics=("arbitrary",), vmem_limit_bytes=VMEM_LIMIT),
        name="mixer_prompt",
    )(xp, gmix, win, convw, sinks, wbr, wout, gffn, wr, br)


def _mixer_sample_kernel(x_ref, gmix_ref, win_ref, convw_ref, st_ref, sinks_ref, ck_ref, cv_ref,
                         wbr0_ref, wbr1_ref, wout_ref, gffn_ref, wr_ref, br_ref,
                         h1_in, m_in, topi_in, gate_in,
                         h1_ref, m_ref, topi_ref, gate_ref, conv_ref, kout_ref, vout_ref,
                         a_ref, yc_ref, q_ref, knew_ref, vnew_ref, ya_ref, bias_ref,
                         *, n_seq, n_new):
    del h1_in, m_in, topi_in, gate_in
    g = pl.program_id(0)
    rows_c = SEQ_GROUP * PAST_WB
    rows_n = SEQ_GROUP * n_new
    q_rows = GROUP * rows_n

    @pl.when(g == 0)
    def _():
        x = x_ref[...]
        a = _rms(x, gmix_ref[...]).astype(BF16)
        a_ref[...] = a
        zc = _dot(a, win_ref[:, 0:C_Q])
        u = zc[:, CONV_W:2 * CONV_W] * zc[:, 2 * CONV_W:3 * CONV_W]
        up = [st_ref[0], st_ref[1]] + [u[t * n_seq:(t + 1) * n_seq] for t in range(n_new)]
        cw = convw_ref[...]
        for t in range(n_new):
            y = zc[t * n_seq:(t + 1) * n_seq, 0:CONV_W] * (
                cw[0:1] * up[t] + cw[1:2] * up[t + 1] + cw[2:3] * up[t + 2])
            yc_ref[t * n_seq:(t + 1) * n_seq, :] = y.astype(BF16)
        conv_ref[0] = up[n_new]
        conv_ref[1] = up[n_new + 1]
        q_ref[...] = _dot(a, win_ref[:, C_Q:CS_KV]) * ATT_SCALE
        zkv = _dot(a, win_ref[:, CS_KV:CS_GL])
        knew_ref[...] = zkv[:, 0:KV_W]
        vnew_ref[...] = zkv[:, KV_W:]
        r = lax.broadcasted_iota(I32, (q_rows, rows_c + rows_n), 0)
        c = lax.broadcasted_iota(I32, (q_rows, rows_c + rows_n), 1)
        hl = _div(r, rows_n)
        t = _div(_mod(r, rows_n), SEQ_GROUP)
        j = _mod(r, SEQ_GROUP)
        is_cache = c < rows_c
        cn = jnp.maximum(c - rows_c, 0)
        kseq = jnp.where(is_cache, _div(c, PAST_WB), _mod(cn, SEQ_GROUP))
        dist = jnp.where(is_cache, PAST_WB + t - _mod(c, PAST_WB), t - _div(cn, SEQ_GROUP))
        valid = (kseq == j) & (dist >= 0) & (dist < WINDOW)
        distf = dist.astype(F32)
        for h2 in range(N_KV):
            slope = jnp.zeros(r.shape, F32)
            for i in range(GROUP):
                slope = jnp.where(hl == i, _slope(h2 * GROUP + i), slope)
            bias_ref[h2] = jnp.where(valid, -(slope * distf), NEG_INF)

    base = pl.multiple_of(g * SEQ_GROUP, SEQ_GROUP)
    kc = ck_ref[...]
    vc = cv_ref[...]
    new_rows = lambda ref: [ref[pl.ds(t * n_seq + base, SEQ_GROUP), :] for t in range(n_new)]
    kgrp = jnp.concatenate([kc] + new_rows(knew_ref), axis=0).astype(BF16)
    vgrp = jnp.concatenate([vc] + new_rows(vnew_ref), axis=0).astype(BF16)
    hl_row = _div(lax.broadcasted_iota(I32, (q_rows, 1), 0), rows_n)
    for h2 in range(N_KV):
        qs = jnp.concatenate(
            [q_ref[pl.ds(t * n_seq + base, SEQ_GROUP), (h2 * GROUP + i) * LANES:(h2 * GROUP + i + 1) * LANES]
             for i in range(GROUP) for t in range(n_new)], axis=0).astype(BF16)
        s = _dot_nt(qs, kgrp) + bias_ref[h2]
        sink = jnp.zeros((q_rows, 1), F32)
        for i in range(GROUP):
            sink = jnp.where(hl_row == i, sinks_ref[h2 * GROUP + i], sink)
        mrow = jnp.maximum(jnp.max(s, axis=-1, keepdims=True), sink)
        p = jnp.exp(s - mrow)
        den = jnp.sum(p, axis=-1, keepdims=True) + jnp.exp(sink - mrow)
        o = _dot(p.astype(BF16), vgrp) * (1.0 / den)
        for i in range(GROUP):
            for t in range(n_new):
                r0 = (i * n_new + t) * SEQ_GROUP
                ya_ref[pl.ds(t * n_seq + base, SEQ_GROUP),
                       (h2 * GROUP + i) * LANES:(h2 * GROUP + i + 1) * LANES] = o[r0:r0 + SEQ_GROUP]

    for cache, new_ref, out in ((kc, knew_ref, kout_ref), (vc, vnew_ref, vout_ref)):
        out[...] = pltpu.roll(cache, rows_c - n_new, axis=0)
        for j in range(SEQ_GROUP):
            for t in range(n_new):
                out[pl.ds(j * PAST_WB + PAST_WB - n_new + t, 1), :] = (
                    new_ref[pl.ds(t * n_seq + base + j, 1), :])

    @pl.when(g == pl.num_programs(0) - 1)
    def _():
        h1 = _merge_out(x_ref[...], a_ref[...], yc_ref[...], ya_ref[...].astype(BF16),
                        win_ref[:, CS_GL:IN_WS], wbr0_ref[...], wbr1_ref[...], wout_ref)
        h1_ref[...] = h1
        m = _rms(h1, gffn_ref[...])
        m_ref[...] = m
        topi, gate = _router(m, wr_ref, br_ref)
        topi_ref[...] = topi
        gate_ref[...] = gate


def _mixer_sample(xs, gmix, win_s, convw, state, sinks, ck, cv, wbr0, wbr1e, wout, gffn, wr, br,
                  h1_all, m_all, topi_all, gate_all, n_seq, n_new):
    n_tok = xs.shape[0]
    n_total = h1_all.shape[0]
    blk0 = (n_total - n_tok) // n_tok
    rows_c = SEQ_GROUP * PAST_WB
    rows_n = SEQ_GROUP * n_new
    any_spec = pl.BlockSpec(memory_space=pl.ANY)
    tok_spec = lambda w: pl.BlockSpec((n_tok, w), lambda g: (blk0, 0))
    cache_spec = pl.BlockSpec((rows_c, KV_W), lambda g: (g, 0))
    return pl.pallas_call(
        functools.partial(_mixer_sample_kernel, n_seq=n_seq, n_new=n_new),
        grid=(n_seq // SEQ_GROUP,),
        in_specs=[
            _const_spec((n_tok, D)), _const_spec((1, D)), _const_spec((D, IN_WS)),
            _const_spec((3, CONV_W)), _const_spec((2, n_seq, CONV_W)),
            pl.BlockSpec(memory_space=pltpu.SMEM), cache_spec, cache_spec,
            _const_spec((CONV_W, D)), _const_spec((N_HEADS * LANES, D)), _const_spec((D, D)),
            _const_spec((1, D)), _const_spec((D, LANES)), _const_spec((1, LANES)),
            any_spec, any_spec, any_spec, any_spec,
        ],
        out_specs=[
            tok_spec(D), tok_spec(D), tok_spec(LANES), tok_spec(LANES),
            _const_spec((2, n_seq, CONV_W)), cache_spec, cache_spec,
        ],
        out_shape=[
            jax.ShapeDtypeStruct(h1_all.shape, F32), jax.ShapeDtypeStruct(m_all.shape, F32),
            jax.ShapeDtypeStruct(topi_all.shape, I32), jax.ShapeDtypeStruct(gate_all.shape, F32),
            jax.ShapeDtypeStruct((2, n_seq, CONV_W), F32),
            jax.ShapeDtypeStruct(ck.shape, F32), jax.ShapeDtypeStruct(cv.shape, F32),
        ],
        input_output_aliases={14: 0, 15: 1, 16: 2, 17: 3},
        scratch_shapes=[
            pltpu.VMEM((n_tok, D), BF16),
            pltpu.VMEM((n_tok, CONV_W), BF16),
            pltpu.VMEM((n_tok, N_HEADS * LANES), F32),
            pltpu.VMEM((n_tok, KV_W), F32), pltpu.VMEM((n_tok, KV_W), F32),
            pltpu.VMEM((n_tok, N_HEADS * LANES), F32),
            pltpu.VMEM((N_KV, GROUP * rows_n, rows_c + rows_n), F32),
        ],
        compiler_params=pltpu.CompilerParams(
            dimension_semantics=("arbitrary",), vmem_limit_bytes=VMEM_LIMIT),
        name="mixer_sample",
    )(xs, gmix, win_s, convw, state, sinks, ck, cv, wbr0, wbr1e, wout, gffn, wr, br,
      h1_all, m_all, topi_all, gate_all)


def _lane_cumsum(x):
    lane = lax.broadcasted_iota(I32, x.shape, 1)
    s = 1
    while s < LANES:
        x = x + jnp.where(lane >= s, pltpu.roll(x, s, axis=1), 0)
        s *= 2
    return x


def _route_kernel(topi_ref, slot_ref, tab_ref, meta_ref, cnt_ref, run_ref, pstart_ref, ltri_ref,
                  *, n_blocks_pad):
    p = pl.program_id(0)
    i = pl.program_id(1)
    lane = lax.broadcasted_iota(I32, (TM, LANES), 1)
    ti = topi_ref[...]
    e = [ti[:, k:k + 1] for k in range(TOP_K)]
    hot = jnp.zeros((TM, LANES), F32)
    for k in range(TOP_K):
        hot = hot + jnp.where(lane == e[k], 1.0, 0.0)
    colsum = jnp.sum(hot, axis=0, keepdims=True)

    @pl.when((p == 0) & (i == 0))
    def _():
        cnt_ref[...] = jnp.zeros_like(cnt_ref)
        r = lax.broadcasted_iota(I32, (TM, TM), 0)
        c = lax.broadcasted_iota(I32, (TM, TM), 1)
        ltri_ref[...] = jnp.where(r > c, 1.0, 0.0).astype(BF16)

    @pl.when(p == 0)
    def _():
        cnt_ref[...] += jnp.broadcast_to(colsum, cnt_ref.shape)

    @pl.when((p == 1) & (i == 0))
    def _():
        cnt = cnt_ref[...].astype(I32)
        nblk = _div(cnt + (BLK - 1), BLK)
        pend = _lane_cumsum(nblk)
        pstart = pend - nblk
        pstart_ref[...] = (pstart * BLK).astype(F32)
        run_ref[...] = jnp.zeros_like(run_ref)
        n_used = jnp.max(pend.astype(F32), axis=1, keepdims=True).astype(I32)
        brow = lax.broadcasted_iota(I32, (n_blocks_pad, LANES), 0)
        blane = lax.broadcasted_iota(I32, (n_blocks_pad, LANES), 1)
        done = jnp.where((pend[0:1] <= brow) & (blane < N_EXPERTS), 1.0, 0.0)
        block_e = jnp.minimum(jnp.sum(done, axis=1, keepdims=True).astype(I32), N_EXPERTS - 1)
        tab_ref[...] = jnp.broadcast_to(block_e, tab_ref.shape)
        mrow = lax.broadcasted_iota(I32, (SUBLANES, LANES), 0)
        meta = jnp.where(mrow == 0, cnt,
                         jnp.where(mrow == 1, pstart * BLK,
                                   jnp.where(mrow == 2, pend * BLK, n_used)))
        meta_ref[...] = meta

    @pl.when(p == 1)
    def _():
        before = _dot(ltri_ref[...], hot.astype(BF16))
        pos = pstart_ref[0:1] + run_ref[0:1] + before
        slot = jnp.zeros((TM, LANES), I32)
        for k in range(TOP_K):
            sk = jnp.sum(jnp.where(lane == e[k], pos, 0.0), axis=1, keepdims=True)
            slot = jnp.where(lane == k, sk.astype(I32), slot)
        slot_ref[...] = slot
        run_ref[...] += jnp.broadcast_to(colsum, run_ref.shape)


def _route(topi_all, n_blocks):
    n_total = topi_all.shape[0]
    n_blocks_pad = -(-n_blocks // SUBLANES) * SUBLANES
    stat = lambda: pltpu.VMEM((SUBLANES, LANES), F32)
    return pl.pallas_call(
        functools.partial(_route_kernel, n_blocks_pad=n_blocks_pad),
        grid=(2, n_total // TM),
        in_specs=[pl.BlockSpec((TM, LANES), lambda p, i: (i, 0))],
        out_specs=[
            pl.BlockSpec((TM, LANES), lambda p, i: (i * p, 0)),
            pl.BlockSpec((n_blocks_pad, LANES), lambda p, i: (0, 0)),
            pl.BlockSpec((SUBLANES, LANES), lambda p, i: (0, 0)),
        ],
        out_shape=[
            jax.ShapeDtypeStruct((n_total, LANES), I32),
            jax.ShapeDtypeStruct((n_blocks_pad, LANES), I32),
            jax.ShapeDtypeStruct((SUBLANES, LANES), I32),
        ],
        scratch_shapes=[stat(), stat(), stat(), pltpu.VMEM((TM, TM), BF16)],
        compiler_params=pltpu.CompilerParams(dimension_semantics=("arbitrary", "arbitrary")),
        name="route",
    )(topi_all)


TOKEN_BITS = 15


def _invert_kernel(slot_ref, meta_ref, inv_ref, *, n_tokens):
    n_assign = TOP_K * n_tokens

    def pack(row, tok):
        return lax.shift_left(jnp.asarray(row, I32), jnp.int32(TOKEN_BITS)) | jnp.asarray(tok, I32)

    @pl.when(pl.program_id(0) == 0)
    def _():
        def expert(e, seen):
            seen = seen + meta_ref[e]

            def pad(s, carry):
                inv_ref[s] = pack(n_assign + s - seen, 0)
                return carry
            lax.fori_loop(meta_ref[N_EXPERTS + e] + meta_ref[e], meta_ref[2 * N_EXPERTS + e], pad, 0)
            return seen
        lax.fori_loop(0, N_EXPERTS, expert, 0)

        def unused(s, carry):
            inv_ref[s] = pack(s, 0)
            return carry
        lax.fori_loop(meta_ref[3 * N_EXPERTS] * BLK, inv_ref.shape[0], unused, 0)

    tok0 = pl.program_id(0) * TM
    for k in range(TOP_K):
        def real(t, carry):
            tok = tok0 + t
            row = k * n_tokens + tok
            inv_ref[slot_ref[row]] = pack(row, tok)
            return carry
        lax.fori_loop(0, TM, real, 0, unroll=8)


def _invert(slot_kmajor, meta_flat, n_rows):
    n_tokens = slot_kmajor.shape[0] // TOP_K
    assert n_tokens <= 1 << TOKEN_BITS and n_rows <= 1 << (32 - TOKEN_BITS)
    return pl.pallas_call(
        functools.partial(_invert_kernel, n_tokens=n_tokens),
        grid_spec=pltpu.PrefetchScalarGridSpec(
            num_scalar_prefetch=2, grid=(n_tokens // TM,), in_specs=[],
            out_specs=pl.BlockSpec(memory_space=pltpu.SMEM)),
        out_shape=jax.ShapeDtypeStruct((n_rows,), I32),
        compiler_params=pltpu.CompilerParams(dimension_semantics=("arbitrary",)),
        name="invert",
    )(slot_kmajor, meta_flat)


GLU_GROUP = 2 * LANES


def _gather_rows(inv_ref, m_hbm, xbuf, sem, blk):
    for r in range(BLK):
        tok = inv_ref[blk * BLK + r] & ((1 << TOKEN_BITS) - 1)
        pltpu.make_async_copy(m_hbm.at[pl.ds(tok, 1), :], xbuf.at[pl.ds(r, 1), :], sem).start()


def _scatter_rows(inv_ref, ybuf, out_hbm, sem, blk):
    for r in range(BLK):
        row = lax.shift_right_logical(inv_ref[blk * BLK + r], TOKEN_BITS)
        pltpu.make_async_copy(ybuf.at[pl.ds(r, 1), :], out_hbm.at[pl.ds(row, 1), :], sem).start()


def _moe_kernel(inv_ref, be_ref, meta_ref, m_hbm, wup_hbm, bup_ref, wdn_hbm, bdn_ref, out_hbm,
                xbuf0, xbuf1, ybuf0, ybuf1, wup_st, wdn_st, wup_bf, wdn_bf, perm_ref, zbuf_ref,
                gsem, ssem, wsem, zsem):
    j = pl.program_id(0)
    n_used = meta_ref[3 * N_EXPERTS]
    xbufs, ybufs = (xbuf0, xbuf1), (ybuf0, ybuf1)

    def fetch_weights(e):
        pltpu.make_async_copy(wup_hbm.at[e], wup_st, wsem.at[0]).start()
        pltpu.make_async_copy(wdn_hbm.at[e], wdn_st, wsem.at[1]).start()

    def wait_gather(par):
        pltpu.make_async_copy(m_hbm.at[pl.ds(0, BLK), :], xbufs[par], gsem.at[par]).wait()

    def wait_scatter(par):
        pltpu.make_async_copy(ybufs[par], out_hbm.at[pl.ds(0, BLK), :], ssem.at[par]).wait()

    @pl.when(j == 0)
    def _():
        r = lax.broadcasted_iota(I32, (GLU_GROUP, GLU_GROUP), 0)
        c = lax.broadcasted_iota(I32, (GLU_GROUP, GLU_GROUP), 1)
        src = jnp.where(c < LANES, 2 * c, 2 * (c - LANES) + 1)
        perm_ref[...] = jnp.where(r == src, 1.0, 0.0).astype(BF16)
        zbuf_ref[...] = jnp.zeros_like(zbuf_ref)
        fetch_weights(be_ref[0])
        _gather_rows(inv_ref, m_hbm, xbuf0, gsem.at[0], 0)

    for par in range(2):
        b = 2 * j + par
        other = 1 - par
        active = b < n_used
        e = be_ref[b]

        @pl.when(active & (b >= 2))
        def _():
            wait_scatter(par)

        @pl.when(active)
        def _():
            wait_gather(par)

        @pl.when(active & ((b == 0) | (e != be_ref[jnp.maximum(b - 1, 0)])))
        def _():
            pltpu.make_async_copy(wup_hbm.at[0], wup_st, wsem.at[0]).wait()
            pltpu.make_async_copy(wdn_hbm.at[0], wdn_st, wsem.at[1]).wait()
            for g in range(2 * D_FF // GLU_GROUP):
                wg = wup_st[:, g * GLU_GROUP:(g + 1) * GLU_GROUP].astype(BF16)
                pg = _dot(wg, perm_ref[...]).astype(BF16)
                wup_bf[:, g * LANES:(g + 1) * LANES] = pg[:, 0:LANES]
                wup_bf[:, D_FF + g * LANES:D_FF + (g + 1) * LANES] = pg[:, LANES:]
            wdn_bf[...] = wdn_st[...].astype(BF16)
            next_first = _div(meta_ref[2 * N_EXPERTS + e], BLK)

            @pl.when(next_first < n_used)
            def _():
                fetch_weights(be_ref[next_first])

        def ffn():
            h = _dot(xbufs[par][...].astype(BF16), wup_bf[...]) + bup_ref[pl.ds(e, 1), :]
            glu = jnp.minimum(h[:, 0:D_FF], SWIGLU_LIMIT)
            lin = jnp.clip(h[:, D_FF:], -SWIGLU_LIMIT, SWIGLU_LIMIT)
            act = glu * jax.nn.sigmoid(SWIGLU_ALPHA * glu) * (lin + 1.0)
            ybufs[par][...] = _dot(act.astype(BF16), wdn_bf[...]) + bdn_ref[pl.ds(e, 1), :]

        if par == 0:
            @pl.when(b == 0)
            def _():
                _gather_rows(inv_ref, m_hbm, xbufs[other], gsem.at[other], b + 1)
                ffn()

        @pl.when(active & (b >= 1))
        def _():
            _gather_rows(inv_ref, m_hbm, xbufs[other], gsem.at[other], b + 1)
            _scatter_rows(inv_ref, ybufs[other], out_hbm, ssem.at[other], b - 1)
            ffn()

        @pl.when(b == n_used)
        def _():
            wait_gather(par)
            _scatter_rows(inv_ref, ybufs[other], out_hbm, ssem.at[other], b - 1)
            wait_scatter(other)

            @pl.when(b >= 2)
            def _():
                wait_scatter(par)

        @pl.when(b >= n_used)
        def _():
            for half in range(2):
                start = pl.multiple_of(b * BLK + half * (BLK // 2), BLK // 2)
                cp = pltpu.make_async_copy(zbuf_ref, out_hbm.at[pl.ds(start, BLK // 2), :], zsem)
                cp.start()
                cp.wait()


def _moe(inv, block_e, meta_flat, m_all, wup, bup, wdn, bdn):
    n_rows = inv.shape[0]
    any_spec = pl.BlockSpec(memory_space=pl.ANY)
    xy = lambda: pltpu.VMEM((BLK, D), F32)
    return pl.pallas_call(
        _moe_kernel,
        grid_spec=pltpu.PrefetchScalarGridSpec(
            num_scalar_prefetch=3,
            grid=(n_rows // (2 * BLK),),
            in_specs=[
                any_spec, any_spec,
                pl.BlockSpec((N_EXPERTS, 2 * D_FF), lambda j, *_: (0, 0)),
                any_spec,
                pl.BlockSpec((N_EXPERTS, D), lambda j, *_: (0, 0)),
            ],
            out_specs=any_spec,
            scratch_shapes=[
                xy(), xy(), xy(), xy(),
                pltpu.VMEM((D, 2 * D_FF), F32), pltpu.VMEM((D_FF, D), F32),
                pltpu.VMEM((D, 2 * D_FF), BF16), pltpu.VMEM((D_FF, D), BF16),
                pltpu.VMEM((GLU_GROUP, GLU_GROUP), BF16), pltpu.VMEM((BLK // 2, D), F32),
                pltpu.SemaphoreType.DMA((2,)), pltpu.SemaphoreType.DMA((2,)),
                pltpu.SemaphoreType.DMA((2,)), pltpu.SemaphoreType.DMA,
            ],
        ),
        out_shape=jax.ShapeDtypeStruct((n_rows, D), F32),
        compiler_params=pltpu.CompilerParams(
            dimension_semantics=("arbitrary",), vmem_limit_bytes=VMEM_LIMIT),
        name="moe",
    )(inv, block_e, meta_flat, m_all, wup, bup, wdn, bdn)


def _tail_kernel(ye0_ref, ye1_ref, ye2_ref, ye3_ref, gate_ref, h1_ref, pp_ref, pd_ref, gple_ref,
                 wpg_ref, wple_ref, gfin_ref, yp_ref, yd_ref, *, prompt_tiles):
    i = pl.program_id(0)
    gate = gate_ref[...]
    moe = None
    for k, ye_ref in enumerate((ye0_ref, ye1_ref, ye2_ref, ye3_ref)):
        term = gate[:, k:k + 1] * ye_ref[...]
        moe = term if moe is None else moe + term
    h2 = h1_ref[...] + moe
    pg = jax.nn.sigmoid(_dot(_rms(h2, gple_ref[...]).astype(BF16), wpg_ref[...]))
    p = jnp.where(i < prompt_tiles, pp_ref[...], pd_ref[...])
    h3 = h2 + pg * _dot(p.astype(BF16), wple_ref[...])
    y = _rms(h3, gfin_ref[...])

    @pl.when(i < prompt_tiles)
    def _():
        yp_ref[...] = y

    @pl.when(i >= prompt_tiles)
    def _():
        yd_ref[...] = y


def _tail(y_exp, gate_all, h1_all, p_prompt, p_dec, gple, wpg, wple, gfin):
    n_total = h1_all.shape[0]
    n_prompt = p_prompt.shape[0]
    prompt_tiles = n_prompt // TMT
    tiles = n_total // TMT
    row_spec = lambda w: pl.BlockSpec((TMT, w), lambda i: (i, 0))
    plane = lambda k: pl.BlockSpec((TMT, D), lambda i: (k * tiles + i, 0))
    prompt_rows = lambda w: pl.BlockSpec((TMT, w), lambda i: (jnp.minimum(i, prompt_tiles - 1), 0))
    dec_rows = lambda w: pl.BlockSpec((TMT, w), lambda i: (jnp.maximum(i - prompt_tiles, 0), 0))
    return pl.pallas_call(
        functools.partial(_tail_kernel, prompt_tiles=prompt_tiles),
        grid=(tiles,),
        in_specs=[
            plane(0), plane(1), plane(2), plane(3), row_spec(LANES), row_spec(D),
            prompt_rows(PLE_DIM), dec_rows(PLE_DIM),
            _const_spec((1, D)), _const_spec((D, D)), _const_spec((PLE_DIM, D)), _const_spec((1, D)),
        ],
        out_specs=[prompt_rows(D), dec_rows(D)],
        out_shape=[jax.ShapeDtypeStruct((n_prompt, D), F32),
                   jax.ShapeDtypeStruct((n_total - n_prompt, D), F32)],
        compiler_params=pltpu.CompilerParams(
            dimension_semantics=("arbitrary",), vmem_limit_bytes=VMEM_LIMIT),
        name="tail",
    )(y_exp, y_exp, y_exp, y_exp, gate_all, h1_all, p_prompt, p_dec, gple, wpg, wple, gfin)


def _widen_heads(w_q):
    wq = w_q.reshape(D, N_HEADS, HEAD_DIM)
    z = jnp.zeros_like(wq)
    lo = jnp.concatenate([wq, z], axis=-1)
    hi = jnp.concatenate([z, wq], axis=-1)
    kv_of_head = (jnp.arange(N_HEADS) // GROUP)[None, :, None]
    return jnp.where(kv_of_head == 0, lo, hi).reshape(D, N_HEADS * LANES)


def kernel(x_prompt, x_sample, state_conv, cache_k, cache_v, p_prompt, p_sample, g_mix, w_in, conv_w,
           sinks, w_branch, w_out, g_ffn, w_router, b_router, w_up, b_up, w_down, b_down, g_ple,
           w_ple_gate, w_ple, g_final):
    depth = w_in.shape[0]
    assert depth == 1, "single-layer step"
    n_batch, seq, _ = x_prompt.shape
    n_seq, n_new, _ = x_sample.shape
    wb = cache_k.shape[2]
    assert wb == PAST_WB == WINDOW and seq % TM == 0 and n_seq * n_new == TM
    n_prompt = n_batch * seq
    n_dec = n_seq * n_new
    n_total = n_prompt + n_dec
    n_blocks = -(-(n_total * TOP_K) // BLK) + N_EXPERTS
    assert n_blocks % 2 == 0
    n_rows = n_blocks * BLK

    win = w_in[0].astype(BF16)
    w_q = w_in[0][:, C_Q:C_KV]
    win_s = jnp.concatenate([w_in[0][:, :C_Q], _widen_heads(w_q), w_in[0][:, C_KV:]], axis=1).astype(BF16)
    wbr = w_branch[0].astype(BF16)
    wbr1e = jnp.transpose(_widen_heads(jnp.transpose(w_branch[0, 1]))).astype(BF16)
    wout = w_out[0].astype(BF16)
    wr = jnp.pad(w_router[0], ((0, 0), (0, LANES - N_EXPERTS)))
    br = jnp.pad(b_router[0], (0, LANES - N_EXPERTS), constant_values=-1e30).reshape(1, LANES)
    gmix = g_mix[0].reshape(1, D)
    gffn = g_ffn[0].reshape(1, D)
    bu = b_up[0].reshape(N_EXPERTS, D_FF, 2)
    bup = jnp.concatenate([bu[..., 0], bu[..., 1]], axis=-1)
    bdn = b_down[0]

    xp = x_prompt.reshape(n_prompt, D)
    h1_all, m_all, topi_all, gate_all, conv_p, k_p, v_p = _mixer_prompt(
        xp, gmix, win, conv_w[0], sinks[0], wbr, wout, gffn, wr, br, n_total, seq)
    xs_t = jnp.transpose(x_sample, (1, 0, 2)).reshape(n_dec, D)
    state_t = jnp.transpose(state_conv[0], (1, 0, 2))
    ck = cache_k[0].reshape(n_seq * wb, KV_W)
    cv = cache_v[0].reshape(n_seq * wb, KV_W)
    h1_all, m_all, topi_all, gate_all, conv_s, k_s, v_s = _mixer_sample(
        xs_t, gmix, win_s, conv_w[0], state_t, sinks[0], ck, cv, wbr[0], wbr1e, wout, gffn, wr, br,
        h1_all, m_all, topi_all, gate_all, n_seq, n_new)

    slot_w, tab, meta = _route(topi_all, n_blocks)
    slot_kmajor = jnp.transpose(slot_w[:, :TOP_K]).reshape(-1)
    meta_flat = jnp.concatenate([meta[0:3, :N_EXPERTS].reshape(-1), meta[3, 0:1]])
    inv = _invert(slot_kmajor, meta_flat, n_rows)
    y_exp = _moe(inv, tab[:n_blocks, 0], meta_flat, m_all, w_up[0], bup, w_down[0], bdn)

    y_p, y_d = _tail(y_exp, gate_all, h1_all, p_prompt[0].reshape(n_prompt, PLE_DIM),
                     jnp.transpose(p_sample[0], (1, 0, 2)).reshape(n_dec, PLE_DIM),
                     g_ple[0].reshape(1, D), w_ple_gate[0].astype(BF16), w_ple[0].astype(BF16),
                     g_final.reshape(1, D))

    y_prompt = y_p.reshape(n_batch, seq, D)
    y_sample = jnp.transpose(y_d.reshape(n_new, n_seq, D), (1, 0, 2))
    conv_prompt = conv_p[:, SUBLANES - 2:, :][None]
    conv_sample = jnp.transpose(conv_s, (1, 0, 2))[None]
    kv_shape = (1, -1, wb, N_KV, HEAD_DIM)
    return (y_prompt, y_sample, conv_prompt, conv_sample,
            k_p.reshape(kv_shape), v_p.reshape(kv_shape), k_s.reshape(kv_shape), v_s.reshape(kv_shape))
```

```python
import functools

import jax
import jax.numpy as jnp
from jax import lax
from jax.experimental import pallas as pl
from jax.experimental.pallas import tpu as pltpu

F32 = jnp.float32
BF16 = jnp.bfloat16
I32 = jnp.int32

D = 1024
CONV_W = 512
HEAD_DIM = 64
N_HEADS = 8
N_KV = 2
GROUP = N_HEADS // N_KV
ATT_W = N_HEADS * HEAD_DIM
KV_W = N_KV * HEAD_DIM
WINDOW = 128
ATT_SCALE = HEAD_DIM ** -0.5
N_EXPERTS = 32
TOP_K = 4
D_FF = 1024
PLE_DIM = 256
SWIGLU_ALPHA = 1.702
SWIGLU_LIMIT = 7.0
RMS_EPS = 1e-5
PAST_WB = 128

LANES = 128
SUBLANES = 8
VMEM_LIMIT = 56 * 1024 * 1024

TM = 512
TMT = 256
BLK = 256
SEQ_GROUP = 8

C_Q = 3 * CONV_W
C_KV = C_Q + ATT_W
C_GL = C_KV + 2 * KV_W
IN_W = C_GL + 2 * D
CS_KV = C_Q + N_HEADS * LANES
CS_GL = CS_KV + 2 * KV_W
IN_WS = CS_GL + 2 * D

NEG_INF = float("-inf")


def _slope(head):
    return 2.0 ** (-(head + 1))


def _rms(x, g):
    r = lax.rsqrt(jnp.mean(x * x, axis=-1, keepdims=True) + RMS_EPS)
    return x * r * g


def _div(x, n):
    assert n & (n - 1) == 0
    return lax.shift_right_logical(x, n.bit_length() - 1)


def _mod(x, n):
    assert n & (n - 1) == 0
    return x & (n - 1)


def _dot(a, b):
    return jnp.dot(a, b, preferred_element_type=F32)


def _dot_nt(a, b):
    return lax.dot_general(a, b, (((1,), (1,)), ((), ())), preferred_element_type=F32)


def _router(m, wr_ref, br_ref):
    m_hi = m.astype(BF16)
    m_lo = (m - m_hi.astype(F32)).astype(BF16)
    hi_part = _dot(m_hi, wr_ref[...])
    logits = (hi_part[:, 0:LANES] + hi_part[:, LANES:] + _dot(m_lo, wr_ref[:, 0:LANES])) + br_ref[...]
    tm = logits.shape[0]
    lane = lax.broadcasted_iota(I32, (tm, LANES), 1)
    lanef = lane.astype(F32)
    vals, idxs = [], []
    l = logits
    for _ in range(TOP_K):
        mx = jnp.max(l, axis=-1, keepdims=True)
        idx = jnp.min(jnp.where(l == mx, lanef, float(LANES)), axis=-1, keepdims=True)
        vals.append(mx)
        idxs.append(idx.astype(I32))
        l = jnp.where(lanef == idx, NEG_INF, l)
    es = [jnp.exp(v - vals[0]) for v in vals]
    den = es[0] + es[1] + es[2] + es[3]
    topi = jnp.zeros((tm, LANES), I32)
    gate = jnp.zeros((tm, LANES), F32)
    for k in range(TOP_K):
        topi = jnp.where(lane == k, idxs[k], topi)
        gate = jnp.where(lane == k, es[k] / den, gate)
    return topi, gate


def _merge_out(x, a_bf, yc_bf, ya_bf, win_gl, w_br0, w_br1, wout_ref):
    zg = _dot(a_bf, win_gl)
    yb0 = _dot(yc_bf, w_br0)
    yb1 = _dot(ya_bf, w_br1)
    merged = jax.nn.sigmoid(zg[:, :D]) * yb0 + jax.nn.sigmoid(zg[:, D:]) * yb1
    return x + _dot(merged.astype(BF16), wout_ref[...])


def _mixer_prompt_kernel(*refs, tiles_per_seq, n_tiles):
    i = pl.program_id(0)

    @pl.when(i < n_tiles)
    def _():
        _mixer_prompt_tile(*refs, tiles_per_seq=tiles_per_seq)

    @pl.when(i == n_tiles)
    def _():
        for ref in refs[10:14]:
            ref[...] = jnp.zeros_like(ref)


def _mixer_prompt_tile(x_ref, gmix_ref, win_ref, convw_ref, sinks_ref, wbr_ref, wout_ref,
                       gffn_ref, wr_ref, br_ref,
                       h1_ref, m_ref, topi_ref, gate_ref, conv_ref, kst_ref, vst_ref,
                       ucarry_ref, kd_ref, vd_ref, ya_ref, bias_ref, *, tiles_per_seq):
    ti = pl.program_id(0) % tiles_per_seq

    @pl.when(pl.program_id(0) == 0)
    def _():
        qi = _mod(lax.broadcasted_iota(I32, (GROUP * WINDOW, 2 * WINDOW), 0), WINDOW)
        g = _div(lax.broadcasted_iota(I32, (GROUP * WINDOW, 2 * WINDOW), 0), WINDOW)
        dist = qi - lax.broadcasted_iota(I32, (GROUP * WINDOW, 2 * WINDOW), 1) + WINDOW
        distf = dist.astype(F32)
        for h2 in range(N_KV):
            slope = jnp.zeros(dist.shape, F32)
            for i in range(GROUP):
                slope = jnp.where(g == i, _slope(h2 * GROUP + i), slope)
            bias_ref[h2] = jnp.where((dist >= 0) & (dist < WINDOW), -(slope * distf), NEG_INF)

    @pl.when(ti == 0)
    def _():
        ucarry_ref[...] = jnp.zeros_like(ucarry_ref)
        kd_ref[:, 0:WINDOW, :] = jnp.zeros((N_KV, WINDOW, LANES), BF16)
        vd_ref[:, 0:WINDOW, :] = jnp.zeros((N_KV, WINDOW, LANES), BF16)

    x = x_ref[...]
    a = _rms(x, gmix_ref[...]).astype(BF16)

    zc = _dot(a, win_ref[:, 0:C_Q])
    u = zc[:, CONV_W:2 * CONV_W] * zc[:, 2 * CONV_W:3 * CONV_W]
    row = lax.broadcasted_iota(I32, (TM, CONV_W), 0)
    c0 = ucarry_ref[SUBLANES - 2:SUBLANES - 1, :]
    c1 = ucarry_ref[SUBLANES - 1:SUBLANES, :]
    u1 = jnp.where(row == 0, c1, pltpu.roll(u, 1, axis=0))
    u2 = jnp.where(row == 0, c0, jnp.where(row == 1, c1, pltpu.roll(u, 2, axis=0)))
    cw = convw_ref[...]
    yc = zc[:, 0:CONV_W] * (cw[0:1] * u2 + cw[1:2] * u1 + cw[2:3] * u)
    ucarry_ref[...] = u[TM - SUBLANES:, :]
    conv_ref[0] = u[TM - SUBLANES:, :]

    zkv = _dot(a, win_ref[:, C_KV:C_GL])
    k = zkv[:, 0:KV_W]
    v = zkv[:, KV_W:]
    kst_ref[0] = k[TM - WINDOW:, :]
    vst_ref[0] = v[TM - WINDOW:, :]
    lo = lax.broadcasted_iota(I32, (TM, LANES), 1) < HEAD_DIM
    for src, dst in ((k, kd_ref), (v, vd_ref)):
        s_lo = jnp.where(lo, src, 0.0)
        s_hi = jnp.where(lo, 0.0, src)
        dst[0, WINDOW:, :] = (s_lo + pltpu.roll(s_lo, HEAD_DIM, axis=1)).astype(BF16)
        dst[1, WINDOW:, :] = (s_hi + pltpu.roll(s_hi, HEAD_DIM, axis=1)).astype(BF16)

    qf = _dot(a, win_ref[:, C_Q:C_KV]) * ATT_SCALE
    lo_all = (lax.broadcasted_iota(I32, (TM, ATT_W), 1) & (LANES - 1)) < HEAD_DIM
    q_lo = jnp.where(lo_all, qf, 0.0).astype(BF16)
    q_hi = jnp.where(lo_all, 0.0, qf).astype(BF16)
    lo_q = lax.broadcasted_iota(I32, (WINDOW, LANES), 1) < HEAD_DIM
    first_keys_ok = (lax.broadcasted_iota(I32, (1, 2 * WINDOW), 1) + ti * TM) >= WINDOW
    hrow = _div(lax.broadcasted_iota(I32, (GROUP * WINDOW, 1), 0), WINDOW)
    for h2 in range(N_KV):
        sink = jnp.zeros((GROUP * WINDOW, 1), F32)
        for g in range(GROUP):
            sink = jnp.where(hrow == g, sinks_ref[h2 * GROUP + g], sink)
        for b in range(TM // WINDOW):
            rb = b * WINDOW
            cols = [(h2 * (GROUP // 2) + pr) * LANES for pr in range(GROUP // 2)]
            qs = jnp.concatenate(
                [part[rb:rb + WINDOW, c:c + LANES] for c in cols for part in (q_lo, q_hi)], axis=0)
            s = _dot_nt(qs, kd_ref[h2, rb:rb + 2 * WINDOW, :]) + bias_ref[h2]
            if b == 0:
                s = jnp.where(first_keys_ok, s, NEG_INF)
            mrow = jnp.maximum(jnp.max(s, axis=-1, keepdims=True), sink)
            p = jnp.exp(s - mrow)
            den = jnp.sum(p, axis=-1, keepdims=True) + jnp.exp(sink - mrow)
            o = _dot(p.astype(BF16), vd_ref[h2, rb:rb + 2 * WINDOW, :]) * (1.0 / den)
            for pr, c in enumerate(cols):
                r0 = 2 * pr * WINDOW
                ya_ref[rb:rb + WINDOW, c:c + LANES] = jnp.where(
                    lo_q, o[r0:r0 + WINDOW], o[r0 + WINDOW:r0 + 2 * WINDOW]).astype(BF16)
    kd_ref[:, 0:WINDOW, :] = kd_ref[:, TM:TM + WINDOW, :]
    vd_ref[:, 0:WINDOW, :] = vd_ref[:, TM:TM + WINDOW, :]

    h1 = _merge_out(x, a, yc.astype(BF16), ya_ref[...], win_ref[:, C_GL:IN_W],
                    wbr_ref[0], wbr_ref[1], wout_ref)
    h1_ref[...] = h1
    m = _rms(h1, gffn_ref[...])
    m_ref[...] = m
    topi, gate = _router(m, wr_ref, br_ref)
    topi_ref[...] = topi
    gate_ref[...] = gate


def _const_spec(shape):
    return pl.BlockSpec(shape, lambda *_: (0,) * len(shape))


def _mixer_prompt(xp, gmix, win, convw, sinks, wbr, wout, gffn, wr, br, n_total, seq):
    n_prompt = xp.shape[0]
    n_batch = n_prompt // seq
    tiles_per_seq = seq // TM
    n_tiles = n_prompt // TM
    assert n_total == n_prompt + TM
    row_spec = lambda w: pl.BlockSpec((TM, w), lambda i: (i, 0))
    state_spec = lambda r: pl.BlockSpec(
        (1, r, LANES if r == WINDOW else CONV_W),
        lambda i: (jnp.minimum(i // tiles_per_seq, n_batch - 1), 0, 0))
    return pl.pallas_call(
        functools.partial(_mixer_prompt_kernel, tiles_per_seq=tiles_per_seq, n_tiles=n_tiles),
        grid=(n_tiles + 1,),
        in_specs=[
            pl.BlockSpec((TM, D), lambda i: (jnp.minimum(i, n_tiles - 1), 0)),
            _const_spec((1, D)), _const_spec((D, IN_W)), _const_spec((3, CONV_W)),
            pl.BlockSpec(memory_space=pltpu.SMEM),
            _const_spec((2, CONV_W, D)), _const_spec((D, D)), _const_spec((1, D)),
            _const_spec((D, 2 * LANES)), _const_spec((1, LANES)),
        ],
        out_specs=[
            row_spec(D), row_spec(D), row_spec(LANES), row_spec(LANES),
            state_spec(SUBLANES), state_spec(WINDOW), state_spec(WINDOW),
        ],
        out_shape=[
            jax.ShapeDtypeStruct((n_total, D), F32), jax.ShapeDtypeStruct((n_total, D), F32),
            jax.ShapeDtypeStruct((n_total, LANES), I32), jax.ShapeDtypeStruct((n_total, LANES), F32),
            jax.ShapeDtypeStruct((n_batch, SUBLANES, CONV_W), F32),
            jax.ShapeDtypeStruct((n_batch, WINDOW, KV_W), F32),
            jax.ShapeDtypeStruct((n_batch, WINDOW, KV_W), F32),
        ],
        scratch_shapes=[
            pltpu.VMEM((SUBLANES, CONV_W), F32),
            pltpu.VMEM((N_KV, TM + WINDOW, LANES), BF16),
            pltpu.VMEM((N_KV, TM + WINDOW, LANES), BF16),
            pltpu.VMEM((TM, ATT_W), BF16),
            pltpu.VMEM((N_KV, GROUP * WINDOW, 2 * WINDOW), F32),
        ],
        compiler_params=pltpu.CompilerParams(
            dimension_semantics=("arbitrary",), vmem_limit_bytes=VMEM_LIMIT),
        name="mixer_prompt",
    )(xp, gmix, win, convw, sinks, wbr, wout, gffn, wr, br)


def _mixer_sample_kernel(x_ref, gmix_ref, win_ref, convw_ref, st_ref, sinks_ref, ck_ref, cv_ref,
                         wbr0_ref, wbr1_ref, wout_ref, gffn_ref, wr_ref, br_ref,
                         h1_in, m_in, topi_in, gate_in,
                         h1_ref, m_ref, topi_ref, gate_ref, conv_ref, kout_ref, vout_ref,
                         a_ref, yc_ref, q_ref, knew_ref, vnew_ref, ya_ref, bias_ref,
                         *, n_seq, n_new):
    del h1_in, m_in, topi_in, gate_in
    g = pl.program_id(0)
    rows_c = SEQ_GROUP * PAST_WB
    rows_n = SEQ_GROUP * n_new
    q_rows = GROUP * rows_n

    @pl.when(g == 0)
    def _():
        x = x_ref[...]
        a = _rms(x, gmix_ref[...]).astype(BF16)
        a_ref[...] = a
        zc = _dot(a, win_ref[:, 0:C_Q])
        u = zc[:, CONV_W:2 * CONV_W] * zc[:, 2 * CONV_W:3 * CONV_W]
        up = [st_ref[0], st_ref[1]] + [u[t * n_seq:(t + 1) * n_seq] for t in range(n_new)]
        cw = convw_ref[...]
        for t in range(n_new):
            y = zc[t * n_seq:(t + 1) * n_seq, 0:CONV_W] * (
                cw[0:1] * up[t] + cw[1:2] * up[t + 1] + cw[2:3] * up[t + 2])
            yc_ref[t * n_seq:(t + 1) * n_seq, :] = y.astype(BF16)
        conv_ref[0] = up[n_new]
        conv_ref[1] = up[n_new + 1]
        q_ref[...] = _dot(a, win_ref[:, C_Q:CS_KV]) * ATT_SCALE
        zkv = _dot(a, win_ref[:, CS_KV:CS_GL])
        knew_ref[...] = zkv[:, 0:KV_W]
        vnew_ref[...] = zkv[:, KV_W:]
        r = lax.broadcasted_iota(I32, (q_rows, rows_c + rows_n), 0)
        c = lax.broadcasted_iota(I32, (q_rows, rows_c + rows_n), 1)
        hl = _div(r, rows_n)
        t = _div(_mod(r, rows_n), SEQ_GROUP)
        j = _mod(r, SEQ_GROUP)
        is_cache = c < rows_c
        cn = jnp.maximum(c - rows_c, 0)
        kseq = jnp.where(is_cache, _div(c, PAST_WB), _mod(cn, SEQ_GROUP))
        dist = jnp.where(is_cache, PAST_WB + t - _mod(c, PAST_WB), t - _div(cn, SEQ_GROUP))
        valid = (kseq == j) & (dist >= 0) & (dist < WINDOW)
        distf = dist.astype(F32)
        for h2 in range(N_KV):
            slope = jnp.zeros(r.shape, F32)
            for i in range(GROUP):
                slope = jnp.where(hl == i, _slope(h2 * GROUP + i), slope)
            bias_ref[h2] = jnp.where(valid, -(slope * distf), NEG_INF)

    base = pl.multiple_of(g * SEQ_GROUP, SEQ_GROUP)
    kc = ck_ref[...]
    vc = cv_ref[...]
    new_rows = lambda ref: [ref[pl.ds(t * n_seq + base, SEQ_GROUP), :] for t in range(n_new)]
    kgrp = jnp.concatenate([kc] + new_rows(knew_ref), axis=0).astype(BF16)
    vgrp = jnp.concatenate([vc] + new_rows(vnew_ref), axis=0).astype(BF16)
    hl_row = _div(lax.broadcasted_iota(I32, (q_rows, 1), 0), rows_n)
    for h2 in range(N_KV):
        qs = jnp.concatenate(
            [q_ref[pl.ds(t * n_seq + base, SEQ_GROUP), (h2 * GROUP + i) * LANES:(h2 * GROUP + i + 1) * LANES]
             for i in range(GROUP) for t in range(n_new)], axis=0).astype(BF16)
        s = _dot_nt(qs, kgrp) + bias_ref[h2]
        sink = jnp.zeros((q_rows, 1), F32)
        for i in range(GROUP):
            sink = jnp.where(hl_row == i, sinks_ref[h2 * GROUP + i], sink)
        mrow = jnp.maximum(jnp.max(s, axis=-1, keepdims=True), sink)
        p = jnp.exp(s - mrow)
        den = jnp.sum(p, axis=-1, keepdims=True) + jnp.exp(sink - mrow)
        o = _dot(p.astype(BF16), vgrp) * (1.0 / den)
        for i in range(GROUP):
            for t in range(n_new):
                r0 = (i * n_new + t) * SEQ_GROUP
                ya_ref[pl.ds(t * n_seq + base, SEQ_GROUP),
                       (h2 * GROUP + i) * LANES:(h2 * GROUP + i + 1) * LANES] = o[r0:r0 + SEQ_GROUP]

    for cache, new_ref, out in ((kc, knew_ref, kout_ref), (vc, vnew_ref, vout_ref)):
        out[...] = pltpu.roll(cache, rows_c - n_new, axis=0)
        for j in range(SEQ_GROUP):
            for t in range(n_new):
                out[pl.ds(j * PAST_WB + PAST_WB - n_new + t, 1), :] = (
                    new_ref[pl.ds(t * n_seq + base + j, 1), :])

    @pl.when(g == pl.num_programs(0) - 1)
    def _():
        h1 = _merge_out(x_ref[...], a_ref[...], yc_ref[...], ya_ref[...].astype(BF16),
                        win_ref[:, CS_GL:IN_WS], wbr0_ref[...], wbr1_ref[...], wout_ref)
        h1_ref[...] = h1
        m = _rms(h1, gffn_ref[...])
        m_ref[...] = m
        topi, gate = _router(m, wr_ref, br_ref)
        topi_ref[...] = topi
        gate_ref[...] = gate


def _mixer_sample(xs, gmix, win_s, convw, state, sinks, ck, cv, wbr0, wbr1e, wout, gffn, wr, br,
                  h1_all, m_all, topi_all, gate_all, n_seq, n_new):
    n_tok = xs.shape[0]
    n_total = h1_all.shape[0]
    blk0 = (n_total - n_tok) // n_tok
    rows_c = SEQ_GROUP * PAST_WB
    rows_n = SEQ_GROUP * n_new
    any_spec = pl.BlockSpec(memory_space=pl.ANY)
    tok_spec = lambda w: pl.BlockSpec((n_tok, w), lambda g: (blk0, 0))
    cache_spec = pl.BlockSpec((rows_c, KV_W), lambda g: (g, 0))
    return pl.pallas_call(
        functools.partial(_mixer_sample_kernel, n_seq=n_seq, n_new=n_new),
        grid=(n_seq // SEQ_GROUP,),
        in_specs=[
            _const_spec((n_tok, D)), _const_spec((1, D)), _const_spec((D, IN_WS)),
            _const_spec((3, CONV_W)), _const_spec((2, n_seq, CONV_W)),
            pl.BlockSpec(memory_space=pltpu.SMEM), cache_spec, cache_spec,
            _const_spec((CONV_W, D)), _const_spec((N_HEADS * LANES, D)), _const_spec((D, D)),
            _const_spec((1, D)), _const_spec((D, 2 * LANES)), _const_spec((1, LANES)),
            any_spec, any_spec, any_spec, any_spec,
        ],
        out_specs=[
            tok_spec(D), tok_spec(D), tok_spec(LANES), tok_spec(LANES),
            _const_spec((2, n_seq, CONV_W)), cache_spec, cache_spec,
        ],
        out_shape=[
            jax.ShapeDtypeStruct(h1_all.shape, F32), jax.ShapeDtypeStruct(m_all.shape, F32),
            jax.ShapeDtypeStruct(topi_all.shape, I32), jax.ShapeDtypeStruct(gate_all.shape, F32),
            jax.ShapeDtypeStruct((2, n_seq, CONV_W), F32),
            jax.ShapeDtypeStruct(ck.shape, F32), jax.ShapeDtypeStruct(cv.shape, F32),
        ],
        input_output_aliases={14: 0, 15: 1, 16: 2, 17: 3},
        scratch_shapes=[
            pltpu.VMEM((n_tok, D), BF16),
            pltpu.VMEM((n_tok, CONV_W), BF16),
            pltpu.VMEM((n_tok, N_HEADS * LANES), F32),
            pltpu.VMEM((n_tok, KV_W), F32), pltpu.VMEM((n_tok, KV_W), F32),
            pltpu.VMEM((n_tok, N_HEADS * LANES), F32),
            pltpu.VMEM((N_KV, GROUP * rows_n, rows_c + rows_n), F32),
        ],
        compiler_params=pltpu.CompilerParams(
            dimension_semantics=("arbitrary",), vmem_limit_bytes=VMEM_LIMIT),
        name="mixer_sample",
    )(xs, gmix, win_s, convw, state, sinks, ck, cv, wbr0, wbr1e, wout, gffn, wr, br,
      h1_all, m_all, topi_all, gate_all)


def _lane_cumsum(x):
    lane = lax.broadcasted_iota(I32, x.shape, 1)
    s = 1
    while s < LANES:
        x = x + jnp.where(lane >= s, pltpu.roll(x, s, axis=1), 0)
        s *= 2
    return x


def _route_kernel(topi_ref, slot_ref, tab_ref, meta_ref, cnt_ref, run_ref, pstart_ref, ltri_ref,
                  *, n_blocks_pad):
    p = pl.program_id(0)
    i = pl.program_id(1)
    lane = lax.broadcasted_iota(I32, (TM, LANES), 1)
    ti = topi_ref[...]
    e = [ti[:, k:k + 1] for k in range(TOP_K)]
    hot = jnp.zeros((TM, LANES), F32)
    for k in range(TOP_K):
        hot = hot + jnp.where(lane == e[k], 1.0, 0.0)
    colsum = jnp.sum(hot, axis=0, keepdims=True)

    @pl.when((p == 0) & (i == 0))
    def _():
        cnt_ref[...] = jnp.zeros_like(cnt_ref)
        r = lax.broadcasted_iota(I32, (TM, TM), 0)
        c = lax.broadcasted_iota(I32, (TM, TM), 1)
        ltri_ref[...] = jnp.where(r > c, 1.0, 0.0).astype(BF16)

    @pl.when(p == 0)
    def _():
        cnt_ref[...] += jnp.broadcast_to(colsum, cnt_ref.shape)

    @pl.when((p == 1) & (i == 0))
    def _():
        cnt = cnt_ref[...].astype(I32)
        nblk = _div(cnt + (BLK - 1), BLK)
        pend = _lane_cumsum(nblk)
        pstart = pend - nblk
        pstart_ref[...] = (pstart * BLK).astype(F32)
        run_ref[...] = jnp.zeros_like(run_ref)
        n_used = jnp.max(pend.astype(F32), axis=1, keepdims=True).astype(I32)
        brow = lax.broadcasted_iota(I32, (n_blocks_pad, LANES), 0)
        blane = lax.broadcasted_iota(I32, (n_blocks_pad, LANES), 1)
        done = jnp.where((pend[0:1] <= brow) & (blane < N_EXPERTS), 1.0, 0.0)
        block_e = jnp.minimum(jnp.sum(done, axis=1, keepdims=True).astype(I32), N_EXPERTS - 1)
        tab_ref[...] = jnp.broadcast_to(block_e, tab_ref.shape)
        mrow = lax.broadcasted_iota(I32, (SUBLANES, LANES), 0)
        meta = jnp.where(mrow == 0, cnt,
                         jnp.where(mrow == 1, pstart * BLK,
                                   jnp.where(mrow == 2, pend * BLK, n_used)))
        meta_ref[...] = meta

    @pl.when(p == 1)
    def _():
        before = _dot(ltri_ref[...], hot.astype(BF16))
        pos = pstart_ref[0:1] + run_ref[0:1] + before
        slot = jnp.zeros((TM, LANES), I32)
        for k in range(TOP_K):
            sk = jnp.sum(jnp.where(lane == e[k], pos, 0.0), axis=1, keepdims=True)
            slot = jnp.where(lane == k, sk.astype(I32), slot)
        slot_ref[...] = slot
        run_ref[...] += jnp.broadcast_to(colsum, run_ref.shape)


def _route(topi_all, n_blocks):
    n_total = topi_all.shape[0]
    n_blocks_pad = -(-n_blocks // SUBLANES) * SUBLANES
    stat = lambda: pltpu.VMEM((SUBLANES, LANES), F32)
    return pl.pallas_call(
        functools.partial(_route_kernel, n_blocks_pad=n_blocks_pad),
        grid=(2, n_total // TM),
        in_specs=[pl.BlockSpec((TM, LANES), lambda p, i: (i, 0))],
        out_specs=[
            pl.BlockSpec((TM, LANES), lambda p, i: (i * p, 0)),
            pl.BlockSpec((n_blocks_pad, LANES), lambda p, i: (0, 0)),
            pl.BlockSpec((SUBLANES, LANES), lambda p, i: (0, 0)),
        ],
        out_shape=[
            jax.ShapeDtypeStruct((n_total, LANES), I32),
            jax.ShapeDtypeStruct((n_blocks_pad, LANES), I32),
            jax.ShapeDtypeStruct((SUBLANES, LANES), I32),
        ],
        scratch_shapes=[stat(), stat(), stat(), pltpu.VMEM((TM, TM), BF16)],
        compiler_params=pltpu.CompilerParams(dimension_semantics=("arbitrary", "arbitrary")),
        name="route",
    )(topi_all)


TOKEN_BITS = 15


def _invert_kernel(slot_ref, meta_ref, inv_ref, *, n_tokens):
    n_assign = TOP_K * n_tokens

    def pack(row, tok):
        return lax.shift_left(jnp.asarray(row, I32), jnp.int32(TOKEN_BITS)) | jnp.asarray(tok, I32)

    @pl.when(pl.program_id(0) == 0)
    def _():
        def expert(e, seen):
            seen = seen + meta_ref[e]

            def pad(s, carry):
                inv_ref[s] = pack(n_assign + s - seen, 0)
                return carry
            lax.fori_loop(meta_ref[N_EXPERTS + e] + meta_ref[e], meta_ref[2 * N_EXPERTS + e], pad, 0)
            return seen
        lax.fori_loop(0, N_EXPERTS, expert, 0)

        def unused(s, carry):
            inv_ref[s] = pack(s, 0)
            return carry
        lax.fori_loop(meta_ref[3 * N_EXPERTS] * BLK, inv_ref.shape[0], unused, 0)

    tok0 = pl.program_id(0) * TM
    for k in range(TOP_K):
        def real(t, carry):
            tok = tok0 + t
            row = k * n_tokens + tok
            inv_ref[slot_ref[row]] = pack(row, tok)
            return carry
        lax.fori_loop(0, TM, real, 0, unroll=8)


def _invert(slot_kmajor, meta_flat, n_rows):
    n_tokens = slot_kmajor.shape[0] // TOP_K
    assert n_tokens <= 1 << TOKEN_BITS and n_rows <= 1 << (32 - TOKEN_BITS)
    return pl.pallas_call(
        functools.partial(_invert_kernel, n_tokens=n_tokens),
        grid_spec=pltpu.PrefetchScalarGridSpec(
            num_scalar_prefetch=2, grid=(n_tokens // TM,), in_specs=[],
            out_specs=pl.BlockSpec(memory_space=pltpu.SMEM)),
        out_shape=jax.ShapeDtypeStruct((n_rows,), I32),
        compiler_params=pltpu.CompilerParams(dimension_semantics=("arbitrary",)),
        name="invert",
    )(slot_kmajor, meta_flat)


GLU_GROUP = 2 * LANES


def _gather_rows(inv_ref, m_hbm, xbuf, sem, blk):
    for r in range(BLK):
        tok = inv_ref[blk * BLK + r] & ((1 << TOKEN_BITS) - 1)
        pltpu.make_async_copy(m_hbm.at[pl.ds(tok, 1), :], xbuf.at[pl.ds(r, 1), :], sem).start()


def _scatter_rows(inv_ref, ybuf, out_hbm, sem, blk):
    for r in range(BLK):
        row = lax.shift_right_logical(inv_ref[blk * BLK + r], TOKEN_BITS)
        pltpu.make_async_copy(ybuf.at[pl.ds(r, 1), :], out_hbm.at[pl.ds(row, 1), :], sem).start()


def _moe_kernel(inv_ref, be_ref, meta_ref, m_hbm, wup_hbm, bup_ref, wdn_hbm, bdn_ref, out_hbm,
                xbuf0, xbuf1, ybuf0, ybuf1, wup_st, wdn_st, wup_bf, wdn_bf, perm_ref, zbuf_ref,
                gsem, ssem, wsem, zsem):
    j = pl.program_id(0)
    n_used = meta_ref[3 * N_EXPERTS]
    xbufs, ybufs = (xbuf0, xbuf1), (ybuf0, ybuf1)

    def fetch_weights(e):
        pltpu.make_async_copy(wup_hbm.at[e], wup_st, wsem.at[0]).start()
        pltpu.make_async_copy(wdn_hbm.at[e], wdn_st, wsem.at[1]).start()

    def wait_gather(par):
        pltpu.make_async_copy(m_hbm.at[pl.ds(0, BLK), :], xbufs[par], gsem.at[par]).wait()

    def wait_scatter(par):
        pltpu.make_async_copy(ybufs[par], out_hbm.at[pl.ds(0, BLK), :], ssem.at[par]).wait()

    @pl.when(j == 0)
    def _():
        r = lax.broadcasted_iota(I32, (GLU_GROUP, GLU_GROUP), 0)
        c = lax.broadcasted_iota(I32, (GLU_GROUP, GLU_GROUP), 1)
        src = jnp.where(c < LANES, 2 * c, 2 * (c - LANES) + 1)
        perm_ref[...] = jnp.where(r == src, 1.0, 0.0).astype(BF16)
        zbuf_ref[...] = jnp.zeros_like(zbuf_ref)
        fetch_weights(be_ref[0])
        _gather_rows(inv_ref, m_hbm, xbuf0, gsem.at[0], 0)

    for par in range(2):
        b = 2 * j + par
        other = 1 - par
        active = b < n_used
        e = be_ref[b]

        @pl.when(active & (b >= 2))
        def _():
            wait_scatter(par)

        @pl.when(active)
        def _():
            wait_gather(par)
            _gather_rows(inv_ref, m_hbm, xbufs[other], gsem.at[other], b + 1)

        @pl.when(active & (b >= 1))
        def _():
            _scatter_rows(inv_ref, ybufs[other], out_hbm, ssem.at[other], b - 1)

        @pl.when(active & ((b == 0) | (e != be_ref[jnp.maximum(b - 1, 0)])))
        def _():
            pltpu.make_async_copy(wup_hbm.at[0], wup_st, wsem.at[0]).wait()
            pltpu.make_async_copy(wdn_hbm.at[0], wdn_st, wsem.at[1]).wait()
            for g in range(2 * D_FF // GLU_GROUP):
                wg = wup_st[:, g * GLU_GROUP:(g + 1) * GLU_GROUP].astype(BF16)
                pg = _dot(wg, perm_ref[...]).astype(BF16)
                wup_bf[:, g * LANES:(g + 1) * LANES] = pg[:, 0:LANES]
                wup_bf[:, D_FF + g * LANES:D_FF + (g + 1) * LANES] = pg[:, LANES:]
            wdn_bf[...] = wdn_st[...].astype(BF16)
            next_first = _div(meta_ref[2 * N_EXPERTS + e], BLK)

            @pl.when(next_first < n_used)
            def _():
                fetch_weights(be_ref[next_first])

        @pl.when(active)
        def _():
            h = _dot(xbufs[par][...].astype(BF16), wup_bf[...]) + bup_ref[pl.ds(e, 1), :]
            glu = jnp.minimum(h[:, 0:D_FF], SWIGLU_LIMIT)
            lin = jnp.clip(h[:, D_FF:], -SWIGLU_LIMIT, SWIGLU_LIMIT)
            act = glu * jax.nn.sigmoid(SWIGLU_ALPHA * glu) * (lin + 1.0)
            ybufs[par][...] = _dot(act.astype(BF16), wdn_bf[...]) + bdn_ref[pl.ds(e, 1), :]

        @pl.when(b == n_used)
        def _():
            wait_gather(par)
            _scatter_rows(inv_ref, ybufs[other], out_hbm, ssem.at[other], b - 1)
            wait_scatter(other)

            @pl.when(b >= 2)
            def _():
                wait_scatter(par)

        @pl.when(b >= n_used)
        def _():
            for half in range(2):
                start = pl.multiple_of(b * BLK + half * (BLK // 2), BLK // 2)
                cp = pltpu.make_async_copy(zbuf_ref, out_hbm.at[pl.ds(start, BLK // 2), :], zsem)
                cp.start()
                cp.wait()


def _moe(inv, block_e, meta_flat, m_all, wup, bup, wdn, bdn):
    n_rows = inv.shape[0]
    any_spec = pl.BlockSpec(memory_space=pl.ANY)
    xy = lambda: pltpu.VMEM((BLK, D), F32)
    return pl.pallas_call(
        _moe_kernel,
        grid_spec=pltpu.PrefetchScalarGridSpec(
            num_scalar_prefetch=3,
            grid=(n_rows // (2 * BLK),),
            in_specs=[
                any_spec, any_spec,
                pl.BlockSpec((N_EXPERTS, 2 * D_FF), lambda j, *_: (0, 0)),
                any_spec,
                pl.BlockSpec((N_EXPERTS, D), lambda j, *_: (0, 0)),
            ],
            out_specs=any_spec,
            scratch_shapes=[
                xy(), xy(), xy(), xy(),
                pltpu.VMEM((D, 2 * D_FF), F32), pltpu.VMEM((D_FF, D), F32),
                pltpu.VMEM((D, 2 * D_FF), BF16), pltpu.VMEM((D_FF, D), BF16),
                pltpu.VMEM((GLU_GROUP, GLU_GROUP), BF16), pltpu.VMEM((BLK // 2, D), F32),
                pltpu.SemaphoreType.DMA((2,)), pltpu.SemaphoreType.DMA((2,)),
                pltpu.SemaphoreType.DMA((2,)), pltpu.SemaphoreType.DMA,
            ],
        ),
        out_shape=jax.ShapeDtypeStruct((n_rows, D), F32),
        compiler_params=pltpu.CompilerParams(
            dimension_semantics=("arbitrary",), vmem_limit_bytes=VMEM_LIMIT),
        name="moe",
    )(inv, block_e, meta_flat, m_all, wup, bup, wdn, bdn)


def _tail_kernel(ye0_ref, ye1_ref, ye2_ref, ye3_ref, gate_ref, h1_ref, pp_ref, pd_ref, gple_ref,
                 wpg_ref, wple_ref, gfin_ref, yp_ref, yd_ref, *, prompt_tiles):
    i = pl.program_id(0)
    gate = gate_ref[...]
    moe = None
    for k, ye_ref in enumerate((ye0_ref, ye1_ref, ye2_ref, ye3_ref)):
        term = gate[:, k:k + 1] * ye_ref[...]
        moe = term if moe is None else moe + term
    h2 = h1_ref[...] + moe
    pg = jax.nn.sigmoid(_dot(_rms(h2, gple_ref[...]).astype(BF16), wpg_ref[...]))
    p = jnp.where(i < prompt_tiles, pp_ref[...], pd_ref[...])
    h3 = h2 + pg * _dot(p.astype(BF16), wple_ref[...])
    y = _rms(h3, gfin_ref[...])

    @pl.when(i < prompt_tiles)
    def _():
        yp_ref[...] = y

    @pl.when(i >= prompt_tiles)
    def _():
        yd_ref[...] = y


def _tail(y_exp, gate_all, h1_all, p_prompt, p_dec, gple, wpg, wple, gfin):
    n_total = h1_all.shape[0]
    n_prompt = p_prompt.shape[0]
    prompt_tiles = n_prompt // TMT
    tiles = n_total // TMT
    row_spec = lambda w: pl.BlockSpec((TMT, w), lambda i: (i, 0))
    plane = lambda k: pl.BlockSpec((TMT, D), lambda i: (k * tiles + i, 0))
    prompt_rows = lambda w: pl.BlockSpec((TMT, w), lambda i: (jnp.minimum(i, prompt_tiles - 1), 0))
    dec_rows = lambda w: pl.BlockSpec((TMT, w), lambda i: (jnp.maximum(i - prompt_tiles, 0), 0))
    return pl.pallas_call(
        functools.partial(_tail_kernel, prompt_tiles=prompt_tiles),
        grid=(tiles,),
        in_specs=[
            plane(0), plane(1), plane(2), plane(3), row_spec(LANES), row_spec(D),
            prompt_rows(PLE_DIM), dec_rows(PLE_DIM),
            _const_spec((1, D)), _const_spec((D, D)), _const_spec((PLE_DIM, D)), _const_spec((1, D)),
        ],
        out_specs=[prompt_rows(D), dec_rows(D)],
        out_shape=[jax.ShapeDtypeStruct((n_prompt, D), F32),
                   jax.ShapeDtypeStruct((n_total - n_prompt, D), F32)],
        compiler_params=pltpu.CompilerParams(
            dimension_semantics=("arbitrary",), vmem_limit_bytes=VMEM_LIMIT),
        name="tail",
    )(y_exp, y_exp, y_exp, y_exp, gate_all, h1_all, p_prompt, p_dec, gple, wpg, wple, gfin)


def _widen_heads(w_q):
    wq = w_q.reshape(D, N_HEADS, HEAD_DIM)
    z = jnp.zeros_like(wq)
    lo = jnp.concatenate([wq, z], axis=-1)
    hi = jnp.concatenate([z, wq], axis=-1)
    kv_of_head = (jnp.arange(N_HEADS) // GROUP)[None, :, None]
    return jnp.where(kv_of_head == 0, lo, hi).reshape(D, N_HEADS * LANES)


def kernel(x_prompt, x_sample, state_conv, cache_k, cache_v, p_prompt, p_sample, g_mix, w_in, conv_w,
           sinks, w_branch, w_out, g_ffn, w_router, b_router, w_up, b_up, w_down, b_down, g_ple,
           w_ple_gate, w_ple, g_final):
    depth = w_in.shape[0]
    assert depth == 1, "single-layer step"
    n_batch, seq, _ = x_prompt.shape
    n_seq, n_new, _ = x_sample.shape
    wb = cache_k.shape[2]
    assert wb == PAST_WB == WINDOW and seq % TM == 0 and n_seq * n_new == TM
    n_prompt = n_batch * seq
    n_dec = n_seq * n_new
    n_total = n_prompt + n_dec
    n_blocks = -(-(n_total * TOP_K) // BLK) + N_EXPERTS
    assert n_blocks % 2 == 0
    n_rows = n_blocks * BLK

    win = w_in[0].astype(BF16)
    w_q = w_in[0][:, C_Q:C_KV]
    win_s = jnp.concatenate([w_in[0][:, :C_Q], _widen_heads(w_q), w_in[0][:, C_KV:]], axis=1).astype(BF16)
    wbr = w_branch[0].astype(BF16)
    wbr1e = jnp.transpose(_widen_heads(jnp.transpose(w_branch[0, 1]))).astype(BF16)
    wout = w_out[0].astype(BF16)
    wr_f = jnp.pad(w_router[0], ((0, 0), (0, LANES - N_EXPERTS)))
    wr_hi = wr_f.astype(BF16)
    wr = jnp.concatenate([wr_hi, (wr_f - wr_hi.astype(F32)).astype(BF16)], axis=1)
    br = jnp.pad(b_router[0], (0, LANES - N_EXPERTS), constant_values=-1e30).reshape(1, LANES)
    gmix = g_mix[0].reshape(1, D)
    gffn = g_ffn[0].reshape(1, D)
    bu = b_up[0].reshape(N_EXPERTS, D_FF, 2)
    bup = jnp.concatenate([bu[..., 0], bu[..., 1]], axis=-1)
    bdn = b_down[0]

    xp = x_prompt.reshape(n_prompt, D)
    h1_all, m_all, topi_all, gate_all, conv_p, k_p, v_p = _mixer_prompt(
        xp, gmix, win, conv_w[0], sinks[0], wbr, wout, gffn, wr, br, n_total, seq)
    xs_t = jnp.transpose(x_sample, (1, 0, 2)).reshape(n_dec, D)
    state_t = jnp.transpose(state_conv[0], (1, 0, 2))
    ck = cache_k[0].reshape(n_seq * wb, KV_W)
    cv = cache_v[0].reshape(n_seq * wb, KV_W)
    h1_all, m_all, topi_all, gate_all, conv_s, k_s, v_s = _mixer_sample(
        xs_t, gmix, win_s, conv_w[0], state_t, sinks[0], ck, cv, wbr[0], wbr1e, wout, gffn, wr, br,
        h1_all, m_all, topi_all, gate_all, n_seq, n_new)

    slot_w, tab, meta = _route(topi_all, n_blocks)
    slot_kmajor = jnp.transpose(slot_w[:, :TOP_K]).reshape(-1)
    meta_flat = jnp.concatenate([meta[0:3, :N_EXPERTS].reshape(-1), meta[3, 0:1]])
    inv = _invert(slot_kmajor, meta_flat, n_rows)
    y_exp = _moe(inv, tab[:n_blocks, 0], meta_flat, m_all, w_up[0], bup, w_down[0], bdn)

    y_p, y_d = _tail(y_exp, gate_all, h1_all, p_prompt[0].reshape(n_prompt, PLE_DIM),
                     jnp.transpose(p_sample[0], (1, 0, 2)).reshape(n_dec, PLE_DIM),
                     g_ple[0].reshape(1, D), w_ple_gate[0].astype(BF16), w_ple[0].astype(BF16),
                     g_final.reshape(1, D))

    y_prompt = y_p.reshape(n_batch, seq, D)
    y_sample = jnp.transpose(y_d.reshape(n_new, n_seq, D), (1, 0, 2))
    conv_prompt = conv_p[:, SUBLANES - 2:, :][None]
    conv_sample = jnp.transpose(conv_s, (1, 0, 2))[None]
    kv_shape = (1, -1, wb, N_KV, HEAD_DIM)
    return (y_prompt, y_sample, conv_prompt, conv_sample,
            k_p.reshape(kv_shape), v_p.reshape(kv_shape), k_s.reshape(kv_shape), v_s.reshape(kv_shape))
```

```python
import functools

import jax
import jax.numpy as jnp
from jax import lax
from jax.experimental import pallas as pl
from jax.experimental.pallas import tpu as pltpu

F32 = jnp.float32
BF16 = jnp.bfloat16
I32 = jnp.int32

D = 1024
CONV_W = 512
HEAD_DIM = 64
N_HEADS = 8
N_KV = 2
GROUP = N_HEADS // N_KV
ATT_W = N_HEADS * HEAD_DIM
KV_W = N_KV * HEAD_DIM
WINDOW = 128
ATT_SCALE = HEAD_DIM ** -0.5
N_EXPERTS = 32
TOP_K = 4
D_FF = 1024
PLE_DIM = 256
SWIGLU_ALPHA = 1.702
SWIGLU_LIMIT = 7.0
RMS_EPS = 1e-5
PAST_WB = 128

LANES = 128
SUBLANES = 8
VMEM_LIMIT = 56 * 1024 * 1024

TM = 512
TMT = 256
BLK = 256
SEQ_GROUP = 8

C_Q = 3 * CONV_W
C_KV = C_Q + ATT_W
C_GL = C_KV + 2 * KV_W
IN_W = C_GL + 2 * D
CS_KV = C_Q + N_HEADS * LANES
CS_GL = CS_KV + 2 * KV_W
IN_WS = CS_GL + 2 * D

NEG_INF = float("-inf")


def _slope(head):
    return 2.0 ** (-(head + 1))


def _rms(x, g):
    r = lax.rsqrt(jnp.mean(x * x, axis=-1, keepdims=True) + RMS_EPS)
    return x * r * g


def _div(x, n):
    assert n & (n - 1) == 0
    return lax.shift_right_logical(x, n.bit_length() - 1)


def _mod(x, n):
    assert n & (n - 1) == 0
    return x & (n - 1)


ROW_TILES = D // LANES


def _store_row_tiles(ref, x):
    n = x.shape[0]
    for c in range(ROW_TILES):
        ref[pl.ds(c, n, stride=ROW_TILES), :] = x[:, c * LANES:(c + 1) * LANES]


def _load_row_tiles(ref, dtype=F32):
    n = ref.shape[0] // ROW_TILES
    return jnp.concatenate(
        [ref[pl.ds(c, n, stride=ROW_TILES), :].astype(dtype) for c in range(ROW_TILES)], axis=1)


def _dot(a, b):
    return jnp.dot(a, b, preferred_element_type=F32)


def _dot_nt(a, b):
    return lax.dot_general(a, b, (((1,), (1,)), ((), ())), preferred_element_type=F32)


def _router(m, wr_ref, br_ref):
    m_hi = m.astype(BF16)
    m_lo = (m - m_hi.astype(F32)).astype(BF16)
    hi_part = _dot(m_hi, wr_ref[...])
    logits = (hi_part[:, 0:LANES] + hi_part[:, LANES:] + _dot(m_lo, wr_ref[:, 0:LANES])) + br_ref[...]
    tm = logits.shape[0]
    lane = lax.broadcasted_iota(I32, (tm, LANES), 1)
    lanef = lane.astype(F32)
    vals, idxs = [], []
    l = logits
    for _ in range(TOP_K):
        mx = jnp.max(l, axis=-1, keepdims=True)
        idx = jnp.min(jnp.where(l == mx, lanef, float(LANES)), axis=-1, keepdims=True)
        vals.append(mx)
        idxs.append(idx.astype(I32))
        l = jnp.where(lanef == idx, NEG_INF, l)
    es = [jnp.exp(v - vals[0]) for v in vals]
    den = es[0] + es[1] + es[2] + es[3]
    topi = jnp.zeros((tm, LANES), I32)
    gate = jnp.zeros((tm, LANES), F32)
    for k in range(TOP_K):
        topi = jnp.where(lane == k, idxs[k], topi)
        gate = jnp.where(lane == k, es[k] / den, gate)
    return topi, gate


def _merge_out(x, a_bf, yc_bf, ya_bf, win_gl, w_br0, w_br1, wout_ref):
    zg = _dot(a_bf, win_gl)
    yb0 = _dot(yc_bf, w_br0)
    yb1 = _dot(ya_bf, w_br1)
    merged = jax.nn.sigmoid(zg[:, :D]) * yb0 + jax.nn.sigmoid(zg[:, D:]) * yb1
    return x + _dot(merged.astype(BF16), wout_ref[...])


def _mixer_prompt_kernel(*refs, tiles_per_seq, n_tiles):
    i = pl.program_id(0)

    @pl.when(i < n_tiles)
    def _():
        _mixer_prompt_tile(*refs, tiles_per_seq=tiles_per_seq)

    @pl.when(i == n_tiles)
    def _():
        for ref in refs[10:14]:
            ref[...] = jnp.zeros_like(ref)


def _mixer_prompt_tile(x_ref, gmix_ref, win_ref, convw_ref, sinks_ref, wbr_ref, wout_ref,
                       gffn_ref, wr_ref, br_ref,
                       h1_ref, m_ref, topi_ref, gate_ref, conv_ref, kst_ref, vst_ref,
                       ucarry_ref, kd_ref, vd_ref, ya_ref, bias_ref, *, tiles_per_seq):
    ti = pl.program_id(0) % tiles_per_seq

    @pl.when(pl.program_id(0) == 0)
    def _():
        qi = _mod(lax.broadcasted_iota(I32, (GROUP * WINDOW, 2 * WINDOW), 0), WINDOW)
        g = _div(lax.broadcasted_iota(I32, (GROUP * WINDOW, 2 * WINDOW), 0), WINDOW)
        dist = qi - lax.broadcasted_iota(I32, (GROUP * WINDOW, 2 * WINDOW), 1) + WINDOW
        distf = dist.astype(F32)
        for h2 in range(N_KV):
            slope = jnp.zeros(dist.shape, F32)
            for i in range(GROUP):
                slope = jnp.where(g == i, _slope(h2 * GROUP + i), slope)
            bias_ref[h2] = jnp.where((dist >= 0) & (dist < WINDOW), -(slope * distf), NEG_INF)

    @pl.when(ti == 0)
    def _():
        ucarry_ref[...] = jnp.zeros_like(ucarry_ref)
        kd_ref[:, 0:WINDOW, :] = jnp.zeros((N_KV, WINDOW, LANES), BF16)
        vd_ref[:, 0:WINDOW, :] = jnp.zeros((N_KV, WINDOW, LANES), BF16)

    x = x_ref[...]
    a = _rms(x, gmix_ref[...]).astype(BF16)

    zc = _dot(a, win_ref[:, 0:C_Q])
    u = zc[:, CONV_W:2 * CONV_W] * zc[:, 2 * CONV_W:3 * CONV_W]
    row = lax.broadcasted_iota(I32, (TM, CONV_W), 0)
    c0 = ucarry_ref[SUBLANES - 2:SUBLANES - 1, :]
    c1 = ucarry_ref[SUBLANES - 1:SUBLANES, :]
    u1 = jnp.where(row == 0, c1, pltpu.roll(u, 1, axis=0))
    u2 = jnp.where(row == 0, c0, jnp.where(row == 1, c1, pltpu.roll(u, 2, axis=0)))
    cw = convw_ref[...]
    yc = zc[:, 0:CONV_W] * (cw[0:1] * u2 + cw[1:2] * u1 + cw[2:3] * u)
    ucarry_ref[...] = u[TM - SUBLANES:, :]
    conv_ref[0] = u[TM - SUBLANES:, :]

    zkv = _dot(a, win_ref[:, C_KV:C_GL])
    k = zkv[:, 0:KV_W]
    v = zkv[:, KV_W:]
    kst_ref[0] = k[TM - WINDOW:, :]
    vst_ref[0] = v[TM - WINDOW:, :]
    lo = lax.broadcasted_iota(I32, (TM, LANES), 1) < HEAD_DIM
    for src, dst in ((k, kd_ref), (v, vd_ref)):
        s_lo = jnp.where(lo, src, 0.0)
        s_hi = jnp.where(lo, 0.0, src)
        dst[0, WINDOW:, :] = (s_lo + pltpu.roll(s_lo, HEAD_DIM, axis=1)).astype(BF16)
        dst[1, WINDOW:, :] = (s_hi + pltpu.roll(s_hi, HEAD_DIM, axis=1)).astype(BF16)

    qf = _dot(a, win_ref[:, C_Q:C_KV]) * ATT_SCALE
    lo_all = (lax.broadcasted_iota(I32, (TM, ATT_W), 1) & (LANES - 1)) < HEAD_DIM
    q_lo = jnp.where(lo_all, qf, 0.0).astype(BF16)
    q_hi = jnp.where(lo_all, 0.0, qf).astype(BF16)
    lo_q = lax.broadcasted_iota(I32, (WINDOW, LANES), 1) < HEAD_DIM
    first_keys_ok = (lax.broadcasted_iota(I32, (1, 2 * WINDOW), 1) + ti * TM) >= WINDOW
    hrow = _div(lax.broadcasted_iota(I32, (GROUP * WINDOW, 1), 0), WINDOW)
    for h2 in range(N_KV):
        sink = jnp.zeros((GROUP * WINDOW, 1), F32)
        for g in range(GROUP):
            sink = jnp.where(hrow == g, sinks_ref[h2 * GROUP + g], sink)
        for b in range(TM // WINDOW):
            rb = b * WINDOW
            cols = [(h2 * (GROUP // 2) + pr) * LANES for pr in range(GROUP // 2)]
            qs = jnp.concatenate(
                [part[rb:rb + WINDOW, c:c + LANES] for c in cols for part in (q_lo, q_hi)], axis=0)
            s = _dot_nt(qs, kd_ref[h2, rb:rb + 2 * WINDOW, :]) + bias_ref[h2]
            if b == 0:
                s = jnp.where(first_keys_ok, s, NEG_INF)
            mrow = jnp.maximum(jnp.max(s, axis=-1, keepdims=True), sink)
            p = jnp.exp(s - mrow)
            den = jnp.sum(p, axis=-1, keepdims=True) + jnp.exp(sink - mrow)
            o = _dot(p.astype(BF16), vd_ref[h2, rb:rb + 2 * WINDOW, :]) * (1.0 / den)
            for pr, c in enumerate(cols):
                r0 = 2 * pr * WINDOW
                ya_ref[rb:rb + WINDOW, c:c + LANES] = jnp.where(
                    lo_q, o[r0:r0 + WINDOW], o[r0 + WINDOW:r0 + 2 * WINDOW]).astype(BF16)
    kd_ref[:, 0:WINDOW, :] = kd_ref[:, TM:TM + WINDOW, :]
    vd_ref[:, 0:WINDOW, :] = vd_ref[:, TM:TM + WINDOW, :]

    h1 = _merge_out(x, a, yc.astype(BF16), ya_ref[...], win_ref[:, C_GL:IN_W],
                    wbr_ref[0], wbr_ref[1], wout_ref)
    h1_ref[...] = h1
    m = _rms(h1, gffn_ref[...])
    _store_row_tiles(m_ref, m)
    topi, gate = _router(m, wr_ref, br_ref)
    topi_ref[...] = topi
    gate_ref[...] = gate


def _const_spec(shape):
    return pl.BlockSpec(shape, lambda *_: (0,) * len(shape))


def _mixer_prompt(xp, gmix, win, convw, sinks, wbr, wout, gffn, wr, br, n_total, seq):
    n_prompt = xp.shape[0]
    n_batch = n_prompt // seq
    tiles_per_seq = seq // TM
    n_tiles = n_prompt // TM
    assert n_total == n_prompt + TM
    row_spec = lambda w: pl.BlockSpec((TM, w), lambda i: (i, 0))
    state_spec = lambda r: pl.BlockSpec(
        (1, r, LANES if r == WINDOW else CONV_W),
        lambda i: (jnp.minimum(i // tiles_per_seq, n_batch - 1), 0, 0))
    return pl.pallas_call(
        functools.partial(_mixer_prompt_kernel, tiles_per_seq=tiles_per_seq, n_tiles=n_tiles),
        grid=(n_tiles + 1,),
        in_specs=[
            pl.BlockSpec((TM, D), lambda i: (jnp.minimum(i, n_tiles - 1), 0)),
            _const_spec((1, D)), _const_spec((D, IN_W)), _const_spec((3, CONV_W)),
            pl.BlockSpec(memory_space=pltpu.SMEM),
            _const_spec((2, CONV_W, D)), _const_spec((D, D)), _const_spec((1, D)),
            _const_spec((D, 2 * LANES)), _const_spec((1, LANES)),
        ],
        out_specs=[
            row_spec(D), pl.BlockSpec((TM * ROW_TILES, LANES), lambda i: (i, 0)),
            row_spec(LANES), row_spec(LANES),
            state_spec(SUBLANES), state_spec(WINDOW), state_spec(WINDOW),
        ],
        out_shape=[
            jax.ShapeDtypeStruct((n_total, D), F32),
            jax.ShapeDtypeStruct((n_total * ROW_TILES, LANES), F32),
            jax.ShapeDtypeStruct((n_total, LANES), I32), jax.ShapeDtypeStruct((n_total, LANES), F32),
            jax.ShapeDtypeStruct((n_batch, SUBLANES, CONV_W), F32),
            jax.ShapeDtypeStruct((n_batch, WINDOW, KV_W), F32),
            jax.ShapeDtypeStruct((n_batch, WINDOW, KV_W), F32),
        ],
        scratch_shapes=[
            pltpu.VMEM((SUBLANES, CONV_W), F32),
            pltpu.VMEM((N_KV, TM + WINDOW, LANES), BF16),
            pltpu.VMEM((N_KV, TM + WINDOW, LANES), BF16),
            pltpu.VMEM((TM, ATT_W), BF16),
            pltpu.VMEM((N_KV, GROUP * WINDOW, 2 * WINDOW), F32),
        ],
        compiler_params=pltpu.CompilerParams(
            dimension_semantics=("arbitrary",), vmem_limit_bytes=VMEM_LIMIT),
        name="mixer_prompt",
    )(xp, gmix, win, convw, sinks, wbr, wout, gffn, wr, br)


def _mixer_sample_kernel(x_ref, gmix_ref, win_ref, convw_ref, st_ref, sinks_ref, ck_ref, cv_ref,
                         wbr0_ref, wbr1_ref, wout_ref, gffn_ref, wr_ref, br_ref,
                         h1_in, m_in, topi_in, gate_in,
                         h1_ref, m_ref, topi_ref, gate_ref, conv_ref, kout_ref, vout_ref,
                         a_ref, yc_ref, q_ref, knew_ref, vnew_ref, ya_ref, bias_ref,
                         *, n_seq, n_new):
    del h1_in, m_in, topi_in, gate_in
    g = pl.program_id(0)
    rows_c = SEQ_GROUP * PAST_WB
    rows_n = SEQ_GROUP * n_new
    q_rows = GROUP * rows_n

    @pl.when(g == 0)
    def _():
        x = x_ref[...]
        a = _rms(x, gmix_ref[...]).astype(BF16)
        a_ref[...] = a
        zc = _dot(a, win_ref[:, 0:C_Q])
        u = zc[:, CONV_W:2 * CONV_W] * zc[:, 2 * CONV_W:3 * CONV_W]
        up = [st_ref[0], st_ref[1]] + [u[t * n_seq:(t + 1) * n_seq] for t in range(n_new)]
        cw = convw_ref[...]
        for t in range(n_new):
            y = zc[t * n_seq:(t + 1) * n_seq, 0:CONV_W] * (
                cw[0:1] * up[t] + cw[1:2] * up[t + 1] + cw[2:3] * up[t + 2])
            yc_ref[t * n_seq:(t + 1) * n_seq, :] = y.astype(BF16)
        conv_ref[0] = up[n_new]
        conv_ref[1] = up[n_new + 1]
        q_ref[...] = _dot(a, win_ref[:, C_Q:CS_KV]) * ATT_SCALE
        zkv = _dot(a, win_ref[:, CS_KV:CS_GL])
        knew_ref[...] = zkv[:, 0:KV_W]
        vnew_ref[...] = zkv[:, KV_W:]
        r = lax.broadcasted_iota(I32, (q_rows, rows_c + rows_n), 0)
        c = lax.broadcasted_iota(I32, (q_rows, rows_c + rows_n), 1)
        hl = _div(r, rows_n)
        t = _div(_mod(r, rows_n), SEQ_GROUP)
        j = _mod(r, SEQ_GROUP)
        is_cache = c < rows_c
        cn = jnp.maximum(c - rows_c, 0)
        kseq = jnp.where(is_cache, _div(c, PAST_WB), _mod(cn, SEQ_GROUP))
        dist = jnp.where(is_cache, PAST_WB + t - _mod(c, PAST_WB), t - _div(cn, SEQ_GROUP))
        valid = (kseq == j) & (dist >= 0) & (dist < WINDOW)
        distf = dist.astype(F32)
        for h2 in range(N_KV):
            slope = jnp.zeros(r.shape, F32)
            for i in range(GROUP):
                slope = jnp.where(hl == i, _slope(h2 * GROUP + i), slope)
            bias_ref[h2] = jnp.where(valid, -(slope * distf), NEG_INF)

    base = pl.multiple_of(g * SEQ_GROUP, SEQ_GROUP)
    kc = ck_ref[...]
    vc = cv_ref[...]
    new_rows = lambda ref: [ref[pl.ds(t * n_seq + base, SEQ_GROUP), :] for t in range(n_new)]
    kgrp = jnp.concatenate([kc] + new_rows(knew_ref), axis=0).astype(BF16)
    vgrp = jnp.concatenate([vc] + new_rows(vnew_ref), axis=0).astype(BF16)
    hl_row = _div(lax.broadcasted_iota(I32, (q_rows, 1), 0), rows_n)
    for h2 in range(N_KV):
        qs = jnp.concatenate(
            [q_ref[pl.ds(t * n_seq + base, SEQ_GROUP), (h2 * GROUP + i) * LANES:(h2 * GROUP + i + 1) * LANES]
             for i in range(GROUP) for t in range(n_new)], axis=0).astype(BF16)
        s = _dot_nt(qs, kgrp) + bias_ref[h2]
        sink = jnp.zeros((q_rows, 1), F32)
        for i in range(GROUP):
            sink = jnp.where(hl_row == i, sinks_ref[h2 * GROUP + i], sink)
        mrow = jnp.maximum(jnp.max(s, axis=-1, keepdims=True), sink)
        p = jnp.exp(s - mrow)
        den = jnp.sum(p, axis=-1, keepdims=True) + jnp.exp(sink - mrow)
        o = _dot(p.astype(BF16), vgrp) * (1.0 / den)
        for i in range(GROUP):
            for t in range(n_new):
                r0 = (i * n_new + t) * SEQ_GROUP
                ya_ref[pl.ds(t * n_seq + base, SEQ_GROUP),
                       (h2 * GROUP + i) * LANES:(h2 * GROUP + i + 1) * LANES] = o[r0:r0 + SEQ_GROUP]

    for cache, new_ref, out in ((kc, knew_ref, kout_ref), (vc, vnew_ref, vout_ref)):
        out[...] = pltpu.roll(cache, rows_c - n_new, axis=0)
        for j in range(SEQ_GROUP):
            for t in range(n_new):
                out[pl.ds(j * PAST_WB + PAST_WB - n_new + t, 1), :] = (
                    new_ref[pl.ds(t * n_seq + base + j, 1), :])

    @pl.when(g == pl.num_programs(0) - 1)
    def _():
        h1 = _merge_out(x_ref[...], a_ref[...], yc_ref[...], ya_ref[...].astype(BF16),
                        win_ref[:, CS_GL:IN_WS], wbr0_ref[...], wbr1_ref[...], wout_ref)
        h1_ref[...] = h1
        m = _rms(h1, gffn_ref[...])
        _store_row_tiles(m_ref, m)
        topi, gate = _router(m, wr_ref, br_ref)
        topi_ref[...] = topi
        gate_ref[...] = gate


def _mixer_sample(xs, gmix, win_s, convw, state, sinks, ck, cv, wbr0, wbr1e, wout, gffn, wr, br,
                  h1_all, m_all, topi_all, gate_all, n_seq, n_new):
    n_tok = xs.shape[0]
    n_total = h1_all.shape[0]
    blk0 = (n_total - n_tok) // n_tok
    rows_c = SEQ_GROUP * PAST_WB
    rows_n = SEQ_GROUP * n_new
    any_spec = pl.BlockSpec(memory_space=pl.ANY)
    tok_spec = lambda w: pl.BlockSpec((n_tok, w), lambda g: (blk0, 0))
    cache_spec = pl.BlockSpec((rows_c, KV_W), lambda g: (g, 0))
    return pl.pallas_call(
        functools.partial(_mixer_sample_kernel, n_seq=n_seq, n_new=n_new),
        grid=(n_seq // SEQ_GROUP,),
        in_specs=[
            _const_spec((n_tok, D)), _const_spec((1, D)), _const_spec((D, IN_WS)),
            _const_spec((3, CONV_W)), _const_spec((2, n_seq, CONV_W)),
            pl.BlockSpec(memory_space=pltpu.SMEM), cache_spec, cache_spec,
            _const_spec((CONV_W, D)), _const_spec((N_HEADS * LANES, D)), _const_spec((D, D)),
            _const_spec((1, D)), _const_spec((D, 2 * LANES)), _const_spec((1, LANES)),
            any_spec, any_spec, any_spec, any_spec,
        ],
        out_specs=[
            tok_spec(D), pl.BlockSpec((n_tok * ROW_TILES, LANES), lambda g: (blk0, 0)),
            tok_spec(LANES), tok_spec(LANES),
            _const_spec((2, n_seq, CONV_W)), cache_spec, cache_spec,
        ],
        out_shape=[
            jax.ShapeDtypeStruct(h1_all.shape, F32), jax.ShapeDtypeStruct(m_all.shape, F32),
            jax.ShapeDtypeStruct(topi_all.shape, I32), jax.ShapeDtypeStruct(gate_all.shape, F32),
            jax.ShapeDtypeStruct((2, n_seq, CONV_W), F32),
            jax.ShapeDtypeStruct(ck.shape, F32), jax.ShapeDtypeStruct(cv.shape, F32),
        ],
        input_output_aliases={14: 0, 15: 1, 16: 2, 17: 3},
        scratch_shapes=[
            pltpu.VMEM((n_tok, D), BF16),
            pltpu.VMEM((n_tok, CONV_W), BF16),
            pltpu.VMEM((n_tok, N_HEADS * LANES), F32),
            pltpu.VMEM((n_tok, KV_W), F32), pltpu.VMEM((n_tok, KV_W), F32),
            pltpu.VMEM((n_tok, N_HEADS * LANES), F32),
            pltpu.VMEM((N_KV, GROUP * rows_n, rows_c + rows_n), F32),
        ],
        compiler_params=pltpu.CompilerParams(
            dimension_semantics=("arbitrary",), vmem_limit_bytes=VMEM_LIMIT),
        name="mixer_sample",
    )(xs, gmix, win_s, convw, state, sinks, ck, cv, wbr0, wbr1e, wout, gffn, wr, br,
      h1_all, m_all, topi_all, gate_all)


def _lane_cumsum(x):
    lane = lax.broadcasted_iota(I32, x.shape, 1)
    s = 1
    while s < LANES:
        x = x + jnp.where(lane >= s, pltpu.roll(x, s, axis=1), 0)
        s *= 2
    return x


def _route_kernel(topi_ref, slot_ref, tab_ref, meta_ref, cnt_ref, run_ref, pstart_ref, ltri_ref,
                  *, n_blocks_pad):
    p = pl.program_id(0)
    i = pl.program_id(1)
    lane = lax.broadcasted_iota(I32, (TM, LANES), 1)
    ti = topi_ref[...]
    e = [ti[:, k:k + 1] for k in range(TOP_K)]
    hot = jnp.zeros((TM, LANES), F32)
    for k in range(TOP_K):
        hot = hot + jnp.where(lane == e[k], 1.0, 0.0)
    colsum = jnp.sum(hot, axis=0, keepdims=True)

    @pl.when((p == 0) & (i == 0))
    def _():
        cnt_ref[...] = jnp.zeros_like(cnt_ref)
        r = lax.broadcasted_iota(I32, (TM, TM), 0)
        c = lax.broadcasted_iota(I32, (TM, TM), 1)
        ltri_ref[...] = jnp.where(r > c, 1.0, 0.0).astype(BF16)

    @pl.when(p == 0)
    def _():
        cnt_ref[...] += jnp.broadcast_to(colsum, cnt_ref.shape)

    @pl.when((p == 1) & (i == 0))
    def _():
        cnt = cnt_ref[...].astype(I32)
        nblk = _div(cnt + (BLK - 1), BLK)
        pend = _lane_cumsum(nblk)
        pstart = pend - nblk
        pstart_ref[...] = (pstart * BLK).astype(F32)
        run_ref[...] = jnp.zeros_like(run_ref)
        n_used = jnp.max(pend.astype(F32), axis=1, keepdims=True).astype(I32)
        brow = lax.broadcasted_iota(I32, (n_blocks_pad, LANES), 0)
        blane = lax.broadcasted_iota(I32, (n_blocks_pad, LANES), 1)
        done = jnp.where((pend[0:1] <= brow) & (blane < N_EXPERTS), 1.0, 0.0)
        block_e = jnp.minimum(jnp.sum(done, axis=1, keepdims=True).astype(I32), N_EXPERTS - 1)
        tab_ref[...] = jnp.broadcast_to(block_e, tab_ref.shape)
        mrow = lax.broadcasted_iota(I32, (SUBLANES, LANES), 0)
        meta = jnp.where(mrow == 0, cnt,
                         jnp.where(mrow == 1, pstart * BLK,
                                   jnp.where(mrow == 2, pend * BLK, n_used)))
        meta_ref[...] = meta

    @pl.when(p == 1)
    def _():
        before = _dot(ltri_ref[...], hot.astype(BF16))
        pos = pstart_ref[0:1] + run_ref[0:1] + before
        slot = jnp.zeros((TM, LANES), I32)
        for k in range(TOP_K):
            sk = jnp.sum(jnp.where(lane == e[k], pos, 0.0), axis=1, keepdims=True)
            slot = jnp.where(lane == k, sk.astype(I32), slot)
        slot_ref[...] = slot
        run_ref[...] += jnp.broadcast_to(colsum, run_ref.shape)


def _route(topi_all, n_blocks):
    n_total = topi_all.shape[0]
    n_blocks_pad = -(-n_blocks // SUBLANES) * SUBLANES
    stat = lambda: pltpu.VMEM((SUBLANES, LANES), F32)
    return pl.pallas_call(
        functools.partial(_route_kernel, n_blocks_pad=n_blocks_pad),
        grid=(2, n_total // TM),
        in_specs=[pl.BlockSpec((TM, LANES), lambda p, i: (i, 0))],
        out_specs=[
            pl.BlockSpec((TM, LANES), lambda p, i: (i * p, 0)),
            pl.BlockSpec((n_blocks_pad, LANES), lambda p, i: (0, 0)),
            pl.BlockSpec((SUBLANES, LANES), lambda p, i: (0, 0)),
        ],
        out_shape=[
            jax.ShapeDtypeStruct((n_total, LANES), I32),
            jax.ShapeDtypeStruct((n_blocks_pad, LANES), I32),
            jax.ShapeDtypeStruct((SUBLANES, LANES), I32),
        ],
        scratch_shapes=[stat(), stat(), stat(), pltpu.VMEM((TM, TM), BF16)],
        compiler_params=pltpu.CompilerParams(dimension_semantics=("arbitrary", "arbitrary")),
        name="route",
    )(topi_all)


TOKEN_BITS = 15


def _invert_kernel(slot_ref, meta_ref, inv_ref, *, n_tokens):
    n_assign = TOP_K * n_tokens

    def pack(row, tok):
        return lax.shift_left(jnp.asarray(row, I32), jnp.int32(TOKEN_BITS)) | jnp.asarray(tok, I32)

    @pl.when(pl.program_id(0) == 0)
    def _():
        def expert(e, seen):
            seen = seen + meta_ref[e]

            def pad(s, carry):
                inv_ref[s] = pack(n_assign + s - seen, 0)
                return carry
            lax.fori_loop(meta_ref[N_EXPERTS + e] + meta_ref[e], meta_ref[2 * N_EXPERTS + e], pad, 0)
            return seen
        lax.fori_loop(0, N_EXPERTS, expert, 0)

        def unused(s, carry):
            inv_ref[s] = pack(s, 0)
            return carry
        lax.fori_loop(meta_ref[3 * N_EXPERTS] * BLK, inv_ref.shape[0], unused, 0)

    tok0 = pl.program_id(0) * TM
    for k in range(TOP_K):
        def real(t, carry):
            tok = tok0 + t
            row = k * n_tokens + tok
            inv_ref[slot_ref[row]] = pack(row, tok)
            return carry
        lax.fori_loop(0, TM, real, 0, unroll=8)


def _invert(slot_kmajor, meta_flat, n_rows):
    n_tokens = slot_kmajor.shape[0] // TOP_K
    assert n_tokens <= 1 << TOKEN_BITS and n_rows <= 1 << (32 - TOKEN_BITS)
    return pl.pallas_call(
        functools.partial(_invert_kernel, n_tokens=n_tokens),
        grid_spec=pltpu.PrefetchScalarGridSpec(
            num_scalar_prefetch=2, grid=(n_tokens // TM,), in_specs=[],
            out_specs=pl.BlockSpec(memory_space=pltpu.SMEM)),
        out_shape=jax.ShapeDtypeStruct((n_rows,), I32),
        compiler_params=pltpu.CompilerParams(dimension_semantics=("arbitrary",)),
        name="invert",
    )(slot_kmajor, meta_flat)


GLU_GROUP = 2 * LANES


def _row_tile(ref, row):
    return ref.at[pl.ds(pl.multiple_of(row * ROW_TILES, ROW_TILES), ROW_TILES), :]


def _gather_rows(inv_ref, m_hbm, xbuf, sem, blk):
    for r in range(BLK):
        tok = inv_ref[blk * BLK + r] & ((1 << TOKEN_BITS) - 1)
        pltpu.make_async_copy(_row_tile(m_hbm, tok), _row_tile(xbuf, r), sem).start()


def _scatter_rows(inv_ref, ybuf, out_hbm, sem, blk):
    for r in range(BLK):
        row = lax.shift_right_logical(inv_ref[blk * BLK + r], TOKEN_BITS)
        pltpu.make_async_copy(_row_tile(ybuf, r), _row_tile(out_hbm, row), sem).start()


def _moe_kernel(inv_ref, be_ref, meta_ref, m_hbm, wup_hbm, bup_ref, wdn_hbm, bdn_ref, out_hbm,
                xbuf0, xbuf1, ybuf0, ybuf1, wup_st, wdn_st, wup_bf, wdn_bf, perm_ref, zbuf_ref,
                gsem, ssem, wsem, zsem):
    j = pl.program_id(0)
    n_used = meta_ref[3 * N_EXPERTS]
    xbufs, ybufs = (xbuf0, xbuf1), (ybuf0, ybuf1)

    def fetch_weights(e):
        pltpu.make_async_copy(wup_hbm.at[e], wup_st, wsem.at[0]).start()
        pltpu.make_async_copy(wdn_hbm.at[e], wdn_st, wsem.at[1]).start()

    def wait_gather(par):
        pltpu.make_async_copy(m_hbm.at[pl.ds(0, BLK * ROW_TILES), :], xbufs[par], gsem.at[par]).wait()

    def wait_scatter(par):
        pltpu.make_async_copy(ybufs[par], out_hbm.at[pl.ds(0, BLK * ROW_TILES), :], ssem.at[par]).wait()

    @pl.when(j == 0)
    def _():
        r = lax.broadcasted_iota(I32, (GLU_GROUP, GLU_GROUP), 0)
        c = lax.broadcasted_iota(I32, (GLU_GROUP, GLU_GROUP), 1)
        src = jnp.where(c < LANES, 2 * c, 2 * (c - LANES) + 1)
        perm_ref[...] = jnp.where(r == src, 1.0, 0.0).astype(BF16)
        zbuf_ref[...] = jnp.zeros_like(zbuf_ref)
        fetch_weights(be_ref[0])
        _gather_rows(inv_ref, m_hbm, xbuf0, gsem.at[0], 0)

    for par in range(2):
        b = 2 * j + par
        other = 1 - par
        active = b < n_used
        e = be_ref[b]

        @pl.when(active & (b >= 2))
        def _():
            wait_scatter(par)

        @pl.when(active)
        def _():
            wait_gather(par)
            _gather_rows(inv_ref, m_hbm, xbufs[other], gsem.at[other], b + 1)

        @pl.when(active & (b >= 1))
        def _():
            _scatter_rows(inv_ref, ybufs[other], out_hbm, ssem.at[other], b - 1)

        @pl.when(active & ((b == 0) | (e != be_ref[jnp.maximum(b - 1, 0)])))
        def _():
            pltpu.make_async_copy(wup_hbm.at[0], wup_st, wsem.at[0]).wait()
            pltpu.make_async_copy(wdn_hbm.at[0], wdn_st, wsem.at[1]).wait()
            for g in range(2 * D_FF // GLU_GROUP):
                wg = wup_st[:, g * GLU_GROUP:(g + 1) * GLU_GROUP].astype(BF16)
                pg = _dot(wg, perm_ref[...]).astype(BF16)
                wup_bf[:, g * LANES:(g + 1) * LANES] = pg[:, 0:LANES]
                wup_bf[:, D_FF + g * LANES:D_FF + (g + 1) * LANES] = pg[:, LANES:]
            wdn_bf[...] = wdn_st[...].astype(BF16)
            next_first = _div(meta_ref[2 * N_EXPERTS + e], BLK)

            @pl.when(next_first < n_used)
            def _():
                fetch_weights(be_ref[next_first])

        @pl.when(active)
        def _():
            h = _dot(_load_row_tiles(xbufs[par], BF16), wup_bf[...]) + bup_ref[pl.ds(e, 1), :]
            glu = jnp.minimum(h[:, 0:D_FF], SWIGLU_LIMIT)
            lin = jnp.clip(h[:, D_FF:], -SWIGLU_LIMIT, SWIGLU_LIMIT)
            act = glu * jax.nn.sigmoid(SWIGLU_ALPHA * glu) * (lin + 1.0)
            _store_row_tiles(ybufs[par], _dot(act.astype(BF16), wdn_bf[...]) + bdn_ref[pl.ds(e, 1), :])

        @pl.when(b == n_used)
        def _():
            wait_gather(par)
            _scatter_rows(inv_ref, ybufs[other], out_hbm, ssem.at[other], b - 1)
            wait_scatter(other)

            @pl.when(b >= 2)
            def _():
                wait_scatter(par)

        @pl.when(b >= n_used)
        def _():
            half = zbuf_ref.shape[0]
            for part in range(BLK * ROW_TILES // half):
                start = pl.multiple_of(b * (BLK * ROW_TILES) + part * half, half)
                cp = pltpu.make_async_copy(zbuf_ref, out_hbm.at[pl.ds(start, half), :], zsem)
                cp.start()
                cp.wait()


def _moe(inv, block_e, meta_flat, m_all, wup, bup, wdn, bdn):
    n_rows = inv.shape[0]
    any_spec = pl.BlockSpec(memory_space=pl.ANY)
    xy = lambda: pltpu.VMEM((BLK * ROW_TILES, LANES), F32)
    return pl.pallas_call(
        _moe_kernel,
        grid_spec=pltpu.PrefetchScalarGridSpec(
            num_scalar_prefetch=3,
            grid=(n_rows // (2 * BLK),),
            in_specs=[
                any_spec, any_spec,
                pl.BlockSpec((N_EXPERTS, 2 * D_FF), lambda j, *_: (0, 0)),
                any_spec,
                pl.BlockSpec((N_EXPERTS, D), lambda j, *_: (0, 0)),
            ],
            out_specs=any_spec,
            scratch_shapes=[
                xy(), xy(), xy(), xy(),
                pltpu.VMEM((D, 2 * D_FF), F32), pltpu.VMEM((D_FF, D), F32),
                pltpu.VMEM((D, 2 * D_FF), BF16), pltpu.VMEM((D_FF, D), BF16),
                pltpu.VMEM((GLU_GROUP, GLU_GROUP), BF16),
                pltpu.VMEM((BLK * ROW_TILES // 2, LANES), F32),
                pltpu.SemaphoreType.DMA((2,)), pltpu.SemaphoreType.DMA((2,)),
                pltpu.SemaphoreType.DMA((2,)), pltpu.SemaphoreType.DMA,
            ],
        ),
        out_shape=jax.ShapeDtypeStruct((n_rows * ROW_TILES, LANES), F32),
        compiler_params=pltpu.CompilerParams(
            dimension_semantics=("arbitrary",), vmem_limit_bytes=VMEM_LIMIT),
        name="moe",
    )(inv, block_e, meta_flat, m_all, wup, bup, wdn, bdn)


def _tail_kernel(ye0_ref, ye1_ref, ye2_ref, ye3_ref, gate_ref, h1_ref, pp_ref, pd_ref, gple_ref,
                 wpg_ref, wple_ref, gfin_ref, yp_ref, yd_ref, *, prompt_tiles):
    i = pl.program_id(0)
    gate = gate_ref[...]
    moe = None
    for k, ye_ref in enumerate((ye0_ref, ye1_ref, ye2_ref, ye3_ref)):
        term = gate[:, k:k + 1] * _load_row_tiles(ye_ref)
        moe = term if moe is None else moe + term
    h2 = h1_ref[...] + moe
    pg = jax.nn.sigmoid(_dot(_rms(h2, gple_ref[...]).astype(BF16), wpg_ref[...]))
    p = jnp.where(i < prompt_tiles, pp_ref[...], pd_ref[...])
    h3 = h2 + pg * _dot(p.astype(BF16), wple_ref[...])
    y = _rms(h3, gfin_ref[...])

    @pl.when(i < prompt_tiles)
    def _():
        yp_ref[...] = y

    @pl.when(i >= prompt_tiles)
    def _():
        yd_ref[...] = y


def _tail(y_exp, gate_all, h1_all, p_prompt, p_dec, gple, wpg, wple, gfin):
    n_total = h1_all.shape[0]
    n_prompt = p_prompt.shape[0]
    prompt_tiles = n_prompt // TMT
    tiles = n_total // TMT
    row_spec = lambda w: pl.BlockSpec((TMT, w), lambda i: (i, 0))
    plane = lambda k: pl.BlockSpec((TMT * ROW_TILES, LANES), lambda i: (k * tiles + i, 0))
    prompt_rows = lambda w: pl.BlockSpec((TMT, w), lambda i: (jnp.minimum(i, prompt_tiles - 1), 0))
    dec_rows = lambda w: pl.BlockSpec((TMT, w), lambda i: (jnp.maximum(i - prompt_tiles, 0), 0))
    return pl.pallas_call(
        functools.partial(_tail_kernel, prompt_tiles=prompt_tiles),
        grid=(tiles,),
        in_specs=[
            plane(0), plane(1), plane(2), plane(3), row_spec(LANES), row_spec(D),
            prompt_rows(PLE_DIM), dec_rows(PLE_DIM),
            _const_spec((1, D)), _const_spec((D, D)), _const_spec((PLE_DIM, D)), _const_spec((1, D)),
        ],
        out_specs=[prompt_rows(D), dec_rows(D)],
        out_shape=[jax.ShapeDtypeStruct((n_prompt, D), F32),
                   jax.ShapeDtypeStruct((n_total - n_prompt, D), F32)],
        compiler_params=pltpu.CompilerParams(
            dimension_semantics=("arbitrary",), vmem_limit_bytes=VMEM_LIMIT),
        name="tail",
    )(y_exp, y_exp, y_exp, y_exp, gate_all, h1_all, p_prompt, p_dec, gple, wpg, wple, gfin)


def _widen_heads(w_q):
    wq = w_q.reshape(D, N_HEADS, HEAD_DIM)
    z = jnp.zeros_like(wq)
    lo = jnp.concatenate([wq, z], axis=-1)
    hi = jnp.concatenate([z, wq], axis=-1)
    kv_of_head = (jnp.arange(N_HEADS) // GROUP)[None, :, None]
    return jnp.where(kv_of_head == 0, lo, hi).reshape(D, N_HEADS * LANES)


def kernel(x_prompt, x_sample, state_conv, cache_k, cache_v, p_prompt, p_sample, g_mix, w_in, conv_w,
           sinks, w_branch, w_out, g_ffn, w_router, b_router, w_up, b_up, w_down, b_down, g_ple,
           w_ple_gate, w_ple, g_final):
    depth = w_in.shape[0]
    assert depth == 1, "single-layer step"
    n_batch, seq, _ = x_prompt.shape
    n_seq, n_new, _ = x_sample.shape
    wb = cache_k.shape[2]
    assert wb == PAST_WB == WINDOW and seq % TM == 0 and n_seq * n_new == TM
    n_prompt = n_batch * seq
    n_dec = n_seq * n_new
    n_total = n_prompt + n_dec
    n_blocks = -(-(n_total * TOP_K) // BLK) + N_EXPERTS
    assert n_blocks % 2 == 0
    n_rows = n_blocks * BLK

    win = w_in[0].astype(BF16)
    w_q = w_in[0][:, C_Q:C_KV]
    win_s = jnp.concatenate([w_in[0][:, :C_Q], _widen_heads(w_q), w_in[0][:, C_KV:]], axis=1).astype(BF16)
    wbr = w_branch[0].astype(BF16)
    wbr1e = jnp.transpose(_widen_heads(jnp.transpose(w_branch[0, 1]))).astype(BF16)
    wout = w_out[0].astype(BF16)
    wr_f = jnp.pad(w_router[0], ((0, 0), (0, LANES - N_EXPERTS)))
    wr_hi = wr_f.astype(BF16)
    wr = jnp.concatenate([wr_hi, (wr_f - wr_hi.astype(F32)).astype(BF16)], axis=1)
    br = jnp.pad(b_router[0], (0, LANES - N_EXPERTS), constant_values=-1e30).reshape(1, LANES)
    gmix = g_mix[0].reshape(1, D)
    gffn = g_ffn[0].reshape(1, D)
    bu = b_up[0].reshape(N_EXPERTS, D_FF, 2)
    bup = jnp.concatenate([bu[..., 0], bu[..., 1]], axis=-1)
    bdn = b_down[0]

    xp = x_prompt.reshape(n_prompt, D)
    h1_all, m_all, topi_all, gate_all, conv_p, k_p, v_p = _mixer_prompt(
        xp, gmix, win, conv_w[0], sinks[0], wbr, wout, gffn, wr, br, n_total, seq)
    xs_t = jnp.transpose(x_sample, (1, 0, 2)).reshape(n_dec, D)
    state_t = jnp.transpose(state_conv[0], (1, 0, 2))
    ck = cache_k[0].reshape(n_seq * wb, KV_W)
    cv = cache_v[0].reshape(n_seq * wb, KV_W)
    h1_all, m_all, topi_all, gate_all, conv_s, k_s, v_s = _mixer_sample(
        xs_t, gmix, win_s, conv_w[0], state_t, sinks[0], ck, cv, wbr[0], wbr1e, wout, gffn, wr, br,
        h1_all, m_all, topi_all, gate_all, n_seq, n_new)

    slot_w, tab, meta = _route(topi_all, n_blocks)
    slot_kmajor = jnp.transpose(slot_w[:, :TOP_K]).reshape(-1)
    meta_flat = jnp.concatenate([meta[0:3, :N_EXPERTS].reshape(-1), meta[3, 0:1]])
    inv = _invert(slot_kmajor, meta_flat, n_rows)
    y_exp = _moe(inv, tab[:n_blocks, 0], meta_flat, m_all, w_up[0], bup, w_down[0], bdn)

    y_p, y_d = _tail(y_exp, gate_all, h1_all, p_prompt[0].reshape(n_prompt, PLE_DIM),
                     jnp.transpose(p_sample[0], (1, 0, 2)).reshape(n_dec, PLE_DIM),
                     g_ple[0].reshape(1, D), w_ple_gate[0].astype(BF16), w_ple[0].astype(BF16),
                     g_final.reshape(1, D))

    y_prompt = y_p.reshape(n_batch, seq, D)
    y_sample = jnp.transpose(y_d.reshape(n_new, n_seq, D), (1, 0, 2))
    conv_prompt = conv_p[:, SUBLANES - 2:, :][None]
    conv_sample = jnp.transpose(conv_s, (1, 0, 2))[None]
    kv_shape = (1, -1, wb, N_KV, HEAD_DIM)
    return (y_prompt, y_sample, conv_prompt, conv_sample,
            k_p.reshape(kv_shape), v_p.reshape(kv_shape), k_s.reshape(kv_shape), v_s.reshape(kv_shape))
```

```python
import functools

import jax
import jax.numpy as jnp
from jax import lax
from jax.experimental import pallas as pl
from jax.experimental.pallas import tpu as pltpu

F32 = jnp.float32
BF16 = jnp.bfloat16
I32 = jnp.int32

D = 1024
CONV_W = 512
HEAD_DIM = 64
N_HEADS = 8
N_KV = 2
GROUP = N_HEADS // N_KV
ATT_W = N_HEADS * HEAD_DIM
KV_W = N_KV * HEAD_DIM
WINDOW = 128
ATT_SCALE = HEAD_DIM ** -0.5
N_EXPERTS = 32
TOP_K = 4
D_FF = 1024
PLE_DIM = 256
SWIGLU_ALPHA = 1.702
SWIGLU_LIMIT = 7.0
RMS_EPS = 1e-5
PAST_WB = 128

LANES = 128
SUBLANES = 8
VMEM_LIMIT = 56 * 1024 * 1024

TM = 512
TMT = 256
BLK = 256
SEQ_GROUP = 8

C_Q = 3 * CONV_W
C_KV = C_Q + ATT_W
C_GL = C_KV + 2 * KV_W
IN_W = C_GL + 2 * D
CS_KV = C_Q + N_HEADS * LANES
CS_GL = CS_KV + 2 * KV_W
IN_WS = CS_GL + 2 * D

NEG_INF = float("-inf")


def _slope(head):
    return 2.0 ** (-(head + 1))


def _rms(x, g):
    r = lax.rsqrt(jnp.mean(x * x, axis=-1, keepdims=True) + RMS_EPS)
    return x * r * g


def _div(x, n):
    assert n & (n - 1) == 0
    return lax.shift_right_logical(x, n.bit_length() - 1)


def _mod(x, n):
    assert n & (n - 1) == 0
    return x & (n - 1)


ROW_TILES = D // LANES


def _store_row_tiles(ref, x):
    n = x.shape[0]
    for c in range(ROW_TILES):
        ref[pl.ds(c, n, stride=ROW_TILES), :] = x[:, c * LANES:(c + 1) * LANES]


def _load_row_tiles(ref, dtype=F32):
    n = ref.shape[0] // ROW_TILES
    return jnp.concatenate(
        [ref[pl.ds(c, n, stride=ROW_TILES), :].astype(dtype) for c in range(ROW_TILES)], axis=1)


def _dot(a, b):
    return jnp.dot(a, b, preferred_element_type=F32)


def _dot_nt(a, b):
    return lax.dot_general(a, b, (((1,), (1,)), ((), ())), preferred_element_type=F32)


def _router(m, wr_ref, br_ref):
    m_hi = m.astype(BF16)
    m_lo = (m - m_hi.astype(F32)).astype(BF16)
    hi_part = _dot(m_hi, wr_ref[...])
    logits = (hi_part[:, 0:LANES] + hi_part[:, LANES:] + _dot(m_lo, wr_ref[:, 0:LANES])) + br_ref[...]
    tm = logits.shape[0]
    lane = lax.broadcasted_iota(I32, (tm, LANES), 1)
    lanef = lane.astype(F32)
    vals, idxs = [], []
    l = logits
    for _ in range(TOP_K):
        mx = jnp.max(l, axis=-1, keepdims=True)
        idx = jnp.min(jnp.where(l == mx, lanef, float(LANES)), axis=-1, keepdims=True)
        vals.append(mx)
        idxs.append(idx.astype(I32))
        l = jnp.where(lanef == idx, NEG_INF, l)
    es = [jnp.exp(v - vals[0]) for v in vals]
    den = es[0] + es[1] + es[2] + es[3]
    topi = jnp.zeros((tm, LANES), I32)
    gate = jnp.zeros((tm, LANES), F32)
    for k in range(TOP_K):
        topi = jnp.where(lane == k, idxs[k], topi)
        gate = jnp.where(lane == k, es[k] / den, gate)
    return topi, gate


def _merge_out(x, a_bf, yc_bf, ya_bf, win_gl, w_br0, w_br1, wout_ref):
    zg = _dot(a_bf, win_gl)
    yb0 = _dot(yc_bf, w_br0)
    yb1 = _dot(ya_bf, w_br1)
    merged = jax.nn.sigmoid(zg[:, :D]) * yb0 + jax.nn.sigmoid(zg[:, D:]) * yb1
    return x + _dot(merged.astype(BF16), wout_ref[...])


def _mixer_prompt_kernel(*refs, tiles_per_seq, n_tiles):
    i = pl.program_id(0)

    @pl.when(i < n_tiles)
    def _():
        _mixer_prompt_tile(*refs, tiles_per_seq=tiles_per_seq)

    @pl.when(i == n_tiles)
    def _():
        for ref in refs[10:14]:
            ref[...] = jnp.zeros_like(ref)


def _mixer_prompt_tile(x_ref, gmix_ref, win_ref, convw_ref, sinks_ref, wbr_ref, wout_ref,
                       gffn_ref, wr_ref, br_ref,
                       h1_ref, m_ref, topi_ref, gate_ref, conv_ref, kst_ref, vst_ref,
                       ucarry_ref, kd_ref, vd_ref, ya_ref, bias_ref, *, tiles_per_seq):
    ti = pl.program_id(0) % tiles_per_seq

    @pl.when(pl.program_id(0) == 0)
    def _():
        qi = _mod(lax.broadcasted_iota(I32, (GROUP * WINDOW, 2 * WINDOW), 0), WINDOW)
        g = _div(lax.broadcasted_iota(I32, (GROUP * WINDOW, 2 * WINDOW), 0), WINDOW)
        dist = qi - lax.broadcasted_iota(I32, (GROUP * WINDOW, 2 * WINDOW), 1) + WINDOW
        distf = dist.astype(F32)
        for h2 in range(N_KV):
            slope = jnp.zeros(dist.shape, F32)
            for i in range(GROUP):
                slope = jnp.where(g == i, _slope(h2 * GROUP + i), slope)
            bias_ref[h2] = jnp.where((dist >= 0) & (dist < WINDOW), -(slope * distf), NEG_INF)

    @pl.when(ti == 0)
    def _():
        ucarry_ref[...] = jnp.zeros_like(ucarry_ref)
        kd_ref[:, 0:WINDOW, :] = jnp.zeros((N_KV, WINDOW, LANES), BF16)
        vd_ref[:, 0:WINDOW, :] = jnp.zeros((N_KV, WINDOW, LANES), BF16)

    x = x_ref[...]
    a = _rms(x, gmix_ref[...]).astype(BF16)

    zc = _dot(a, win_ref[:, 0:C_Q])
    u = zc[:, CONV_W:2 * CONV_W] * zc[:, 2 * CONV_W:3 * CONV_W]
    row = lax.broadcasted_iota(I32, (TM, CONV_W), 0)
    c0 = ucarry_ref[SUBLANES - 2:SUBLANES - 1, :]
    c1 = ucarry_ref[SUBLANES - 1:SUBLANES, :]
    u1 = jnp.where(row == 0, c1, pltpu.roll(u, 1, axis=0))
    u2 = jnp.where(row == 0, c0, jnp.where(row == 1, c1, pltpu.roll(u, 2, axis=0)))
    cw = convw_ref[...]
    yc = zc[:, 0:CONV_W] * (cw[0:1] * u2 + cw[1:2] * u1 + cw[2:3] * u)
    ucarry_ref[...] = u[TM - SUBLANES:, :]
    conv_ref[0] = u[TM - SUBLANES:, :]

    zkv = _dot(a, win_ref[:, C_KV:C_GL])
    k = zkv[:, 0:KV_W]
    v = zkv[:, KV_W:]
    kst_ref[0] = k[TM - WINDOW:, :]
    vst_ref[0] = v[TM - WINDOW:, :]
    lo = lax.broadcasted_iota(I32, (TM, LANES), 1) < HEAD_DIM
    for src, dst in ((k, kd_ref), (v, vd_ref)):
        s_lo = jnp.where(lo, src, 0.0)
        s_hi = jnp.where(lo, 0.0, src)
        dst[0, WINDOW:, :] = (s_lo + pltpu.roll(s_lo, HEAD_DIM, axis=1)).astype(BF16)
        dst[1, WINDOW:, :] = (s_hi + pltpu.roll(s_hi, HEAD_DIM, axis=1)).astype(BF16)

    qf = _dot(a, win_ref[:, C_Q:C_KV]) * ATT_SCALE
    lo_all = (lax.broadcasted_iota(I32, (TM, ATT_W), 1) & (LANES - 1)) < HEAD_DIM
    q_lo = jnp.where(lo_all, qf, 0.0).astype(BF16)
    q_hi = jnp.where(lo_all, 0.0, qf).astype(BF16)
    lo_q = lax.broadcasted_iota(I32, (WINDOW, LANES), 1) < HEAD_DIM
    first_keys_ok = (lax.broadcasted_iota(I32, (1, 2 * WINDOW), 1) + ti * TM) >= WINDOW
    hrow = _div(lax.broadcasted_iota(I32, (GROUP * WINDOW, 1), 0), WINDOW)
    for h2 in range(N_KV):
        sink = jnp.zeros((GROUP * WINDOW, 1), F32)
        for g in range(GROUP):
            sink = jnp.where(hrow == g, sinks_ref[h2 * GROUP + g], sink)
        for b in range(TM // WINDOW):
            rb = b * WINDOW
            cols = [(h2 * (GROUP // 2) + pr) * LANES for pr in range(GROUP // 2)]
            qs = jnp.concatenate(
                [part[rb:rb + WINDOW, c:c + LANES] for c in cols for part in (q_lo, q_hi)], axis=0)
            s = _dot_nt(qs, kd_ref[h2, rb:rb + 2 * WINDOW, :]) + bias_ref[h2]
            if b == 0:
                s = jnp.where(first_keys_ok, s, NEG_INF)
            mrow = jnp.maximum(jnp.max(s, axis=-1, keepdims=True), sink)
            p = jnp.exp(s - mrow)
            den = jnp.sum(p, axis=-1, keepdims=True) + jnp.exp(sink - mrow)
            o = _dot(p.astype(BF16), vd_ref[h2, rb:rb + 2 * WINDOW, :]) * (1.0 / den)
            for pr, c in enumerate(cols):
                r0 = 2 * pr * WINDOW
                ya_ref[rb:rb + WINDOW, c:c + LANES] = jnp.where(
                    lo_q, o[r0:r0 + WINDOW], o[r0 + WINDOW:r0 + 2 * WINDOW]).astype(BF16)
    kd_ref[:, 0:WINDOW, :] = kd_ref[:, TM:TM + WINDOW, :]
    vd_ref[:, 0:WINDOW, :] = vd_ref[:, TM:TM + WINDOW, :]

    h1 = _merge_out(x, a, yc.astype(BF16), ya_ref[...], win_ref[:, C_GL:IN_W],
                    wbr_ref[0], wbr_ref[1], wout_ref)
    h1_ref[...] = h1
    m = _rms(h1, gffn_ref[...])
    _store_row_tiles(m_ref, m)
    topi, gate = _router(m, wr_ref, br_ref)
    topi_ref[...] = topi
    gate_ref[...] = gate


def _const_spec(shape):
    return pl.BlockSpec(shape, lambda *_: (0,) * len(shape))


def _mixer_prompt(xp, gmix, win, convw, sinks, wbr, wout, gffn, wr, br, n_total, seq):
    n_prompt = xp.shape[0]
    n_batch = n_prompt // seq
    tiles_per_seq = seq // TM
    n_tiles = n_prompt // TM
    assert n_total == n_prompt + TM
    row_spec = lambda w: pl.BlockSpec((TM, w), lambda i: (i, 0))
    state_spec = lambda r: pl.BlockSpec(
        (1, r, LANES if r == WINDOW else CONV_W),
        lambda i: (jnp.minimum(i // tiles_per_seq, n_batch - 1), 0, 0))
    return pl.pallas_call(
        functools.partial(_mixer_prompt_kernel, tiles_per_seq=tiles_per_seq, n_tiles=n_tiles),
        grid=(n_tiles + 1,),
        in_specs=[
            pl.BlockSpec((TM, D), lambda i: (jnp.minimum(i, n_tiles - 1), 0)),
            _const_spec((1, D)), _const_spec((D, IN_W)), _const_spec((3, CONV_W)),
            pl.BlockSpec(memory_space=pltpu.SMEM),
            _const_spec((2, CONV_W, D)), _const_spec((D, D)), _const_spec((1, D)),
            _const_spec((D, 2 * LANES)), _const_spec((1, LANES)),
        ],
        out_specs=[
            row_spec(D), pl.BlockSpec((TM * ROW_TILES, LANES), lambda i: (i, 0)),
            row_spec(LANES), row_spec(LANES),
            state_spec(SUBLANES), state_spec(WINDOW), state_spec(WINDOW),
        ],
        out_shape=[
            jax.ShapeDtypeStruct((n_total, D), F32),
            jax.ShapeDtypeStruct((n_total * ROW_TILES, LANES), F32),
            jax.ShapeDtypeStruct((n_total, LANES), I32), jax.ShapeDtypeStruct((n_total, LANES), F32),
            jax.ShapeDtypeStruct((n_batch, SUBLANES, CONV_W), F32),
            jax.ShapeDtypeStruct((n_batch, WINDOW, KV_W), F32),
            jax.ShapeDtypeStruct((n_batch, WINDOW, KV_W), F32),
        ],
        scratch_shapes=[
            pltpu.VMEM((SUBLANES, CONV_W), F32),
            pltpu.VMEM((N_KV, TM + WINDOW, LANES), BF16),
            pltpu.VMEM((N_KV, TM + WINDOW, LANES), BF16),
            pltpu.VMEM((TM, ATT_W), BF16),
            pltpu.VMEM((N_KV, GROUP * WINDOW, 2 * WINDOW), F32),
        ],
        compiler_params=pltpu.CompilerParams(
            dimension_semantics=("arbitrary",), vmem_limit_bytes=VMEM_LIMIT),
        name="mixer_prompt",
    )(xp, gmix, win, convw, sinks, wbr, wout, gffn, wr, br)


def _mixer_sample_kernel(x_ref, gmix_ref, win_ref, convw_ref, st_ref, sinks_ref, ck_ref, cv_ref,
                         wbr0_ref, wbr1_ref, wout_ref, gffn_ref, wr_ref, br_ref,
                         h1_in, m_in, topi_in, gate_in,
                         h1_ref, m_ref, topi_ref, gate_ref, conv_ref, kout_ref, vout_ref,
                         a_ref, yc_ref, q_ref, knew_ref, vnew_ref, ya_ref, bias_ref,
                         *, n_seq, n_new):
    del h1_in, m_in, topi_in, gate_in
    g = pl.program_id(0)
    rows_c = SEQ_GROUP * PAST_WB
    rows_n = SEQ_GROUP * n_new
    q_rows = GROUP * rows_n

    @pl.when(g == 0)
    def _():
        x = x_ref[...]
        a = _rms(x, gmix_ref[...]).astype(BF16)
        a_ref[...] = a
        zc = _dot(a, win_ref[:, 0:C_Q])
        u = zc[:, CONV_W:2 * CONV_W] * zc[:, 2 * CONV_W:3 * CONV_W]
        up = [st_ref[0], st_ref[1]] + [u[t * n_seq:(t + 1) * n_seq] for t in range(n_new)]
        cw = convw_ref[...]
        for t in range(n_new):
            y = zc[t * n_seq:(t + 1) * n_seq, 0:CONV_W] * (
                cw[0:1] * up[t] + cw[1:2] * up[t + 1] + cw[2:3] * up[t + 2])
            yc_ref[t * n_seq:(t + 1) * n_seq, :] = y.astype(BF16)
        conv_ref[0] = up[n_new]
        conv_ref[1] = up[n_new + 1]
        q_ref[...] = _dot(a, win_ref[:, C_Q:CS_KV]) * ATT_SCALE
        zkv = _dot(a, win_ref[:, CS_KV:CS_GL])
        knew_ref[...] = zkv[:, 0:KV_W]
        vnew_ref[...] = zkv[:, KV_W:]
        r = lax.broadcasted_iota(I32, (q_rows, rows_c + rows_n), 0)
        c = lax.broadcasted_iota(I32, (q_rows, rows_c + rows_n), 1)
        hl = _div(r, rows_n)
        t = _div(_mod(r, rows_n), SEQ_GROUP)
        j = _mod(r, SEQ_GROUP)
        is_cache = c < rows_c
        cn = jnp.maximum(c - rows_c, 0)
        kseq = jnp.where(is_cache, _div(c, PAST_WB), _mod(cn, SEQ_GROUP))
        dist = jnp.where(is_cache, PAST_WB + t - _mod(c, PAST_WB), t - _div(cn, SEQ_GROUP))
        valid = (kseq == j) & (dist >= 0) & (dist < WINDOW)
        distf = dist.astype(F32)
        for h2 in range(N_KV):
            slope = jnp.zeros(r.shape, F32)
            for i in range(GROUP):
                slope = jnp.where(hl == i, _slope(h2 * GROUP + i), slope)
            bias_ref[h2] = jnp.where(valid, -(slope * distf), NEG_INF)

    base = pl.multiple_of(g * SEQ_GROUP, SEQ_GROUP)
    kc = ck_ref[...]
    vc = cv_ref[...]
    new_rows = lambda ref: [ref[pl.ds(t * n_seq + base, SEQ_GROUP), :] for t in range(n_new)]
    kgrp = jnp.concatenate([kc] + new_rows(knew_ref), axis=0).astype(BF16)
    vgrp = jnp.concatenate([vc] + new_rows(vnew_ref), axis=0).astype(BF16)
    hl_row = _div(lax.broadcasted_iota(I32, (q_rows, 1), 0), rows_n)
    for h2 in range(N_KV):
        qs = jnp.concatenate(
            [q_ref[pl.ds(t * n_seq + base, SEQ_GROUP), (h2 * GROUP + i) * LANES:(h2 * GROUP + i + 1) * LANES]
             for i in range(GROUP) for t in range(n_new)], axis=0).astype(BF16)
        s = _dot_nt(qs, kgrp) + bias_ref[h2]
        sink = jnp.zeros((q_rows, 1), F32)
        for i in range(GROUP):
            sink = jnp.where(hl_row == i, sinks_ref[h2 * GROUP + i], sink)
        mrow = jnp.maximum(jnp.max(s, axis=-1, keepdims=True), sink)
        p = jnp.exp(s - mrow)
        den = jnp.sum(p, axis=-1, keepdims=True) + jnp.exp(sink - mrow)
        o = _dot(p.astype(BF16), vgrp) * (1.0 / den)
        for i in range(GROUP):
            for t in range(n_new):
                r0 = (i * n_new + t) * SEQ_GROUP
                ya_ref[pl.ds(t * n_seq + base, SEQ_GROUP),
                       (h2 * GROUP + i) * LANES:(h2 * GROUP + i + 1) * LANES] = o[r0:r0 + SEQ_GROUP]

    for cache, new_ref, out in ((kc, knew_ref, kout_ref), (vc, vnew_ref, vout_ref)):
        out[...] = pltpu.roll(cache, rows_c - n_new, axis=0)
        for j in range(SEQ_GROUP):
            for t in range(n_new):
                out[pl.ds(j * PAST_WB + PAST_WB - n_new + t, 1), :] = (
                    new_ref[pl.ds(t * n_seq + base + j, 1), :])

    @pl.when(g == pl.num_programs(0) - 1)
    def _():
        h1 = _merge_out(x_ref[...], a_ref[...], yc_ref[...], ya_ref[...].astype(BF16),
                        win_ref[:, CS_GL:IN_WS], wbr0_ref[...], wbr1_ref[...], wout_ref)
        h1_ref[...] = h1
        m = _rms(h1, gffn_ref[...])
        _store_row_tiles(m_ref, m)
        topi, gate = _router(m, wr_ref, br_ref)
        topi_ref[...] = topi
        gate_ref[...] = gate


def _mixer_sample(xs, gmix, win_s, convw, state, sinks, ck, cv, wbr0, wbr1e, wout, gffn, wr, br,
                  h1_all, m_all, topi_all, gate_all, n_seq, n_new):
    n_tok = xs.shape[0]
    n_total = h1_all.shape[0]
    blk0 = (n_total - n_tok) // n_tok
    rows_c = SEQ_GROUP * PAST_WB
    rows_n = SEQ_GROUP * n_new
    any_spec = pl.BlockSpec(memory_space=pl.ANY)
    tok_spec = lambda w: pl.BlockSpec((n_tok, w), lambda g: (blk0, 0))
    cache_spec = pl.BlockSpec((rows_c, KV_W), lambda g: (g, 0))
    return pl.pallas_call(
        functools.partial(_mixer_sample_kernel, n_seq=n_seq, n_new=n_new),
        grid=(n_seq // SEQ_GROUP,),
        in_specs=[
            _const_spec((n_tok, D)), _const_spec((1, D)), _const_spec((D, IN_WS)),
            _const_spec((3, CONV_W)), _const_spec((2, n_seq, CONV_W)),
            pl.BlockSpec(memory_space=pltpu.SMEM), cache_spec, cache_spec,
            _const_spec((CONV_W, D)), _const_spec((N_HEADS * LANES, D)), _const_spec((D, D)),
            _const_spec((1, D)), _const_spec((D, 2 * LANES)), _const_spec((1, LANES)),
            any_spec, any_spec, any_spec, any_spec,
        ],
        out_specs=[
            tok_spec(D), pl.BlockSpec((n_tok * ROW_TILES, LANES), lambda g: (blk0, 0)),
            tok_spec(LANES), tok_spec(LANES),
            _const_spec((2, n_seq, CONV_W)), cache_spec, cache_spec,
        ],
        out_shape=[
            jax.ShapeDtypeStruct(h1_all.shape, F32), jax.ShapeDtypeStruct(m_all.shape, F32),
            jax.ShapeDtypeStruct(topi_all.shape, I32), jax.ShapeDtypeStruct(gate_all.shape, F32),
            jax.ShapeDtypeStruct((2, n_seq, CONV_W), F32),
            jax.ShapeDtypeStruct(ck.shape, F32), jax.ShapeDtypeStruct(cv.shape, F32),
        ],
        input_output_aliases={14: 0, 15: 1, 16: 2, 17: 3},
        scratch_shapes=[
            pltpu.VMEM((n_tok, D), BF16),
            pltpu.VMEM((n_tok, CONV_W), BF16),
            pltpu.VMEM((n_tok, N_HEADS * LANES), F32),
            pltpu.VMEM((n_tok, KV_W), F32), pltpu.VMEM((n_tok, KV_W), F32),
            pltpu.VMEM((n_tok, N_HEADS * LANES), F32),
            pltpu.VMEM((N_KV, GROUP * rows_n, rows_c + rows_n), F32),
        ],
        compiler_params=pltpu.CompilerParams(
            dimension_semantics=("arbitrary",), vmem_limit_bytes=VMEM_LIMIT),
        name="mixer_sample",
    )(xs, gmix, win_s, convw, state, sinks, ck, cv, wbr0, wbr1e, wout, gffn, wr, br,
      h1_all, m_all, topi_all, gate_all)


def _lane_cumsum(x):
    lane = lax.broadcasted_iota(I32, x.shape, 1)
    s = 1
    while s < LANES:
        x = x + jnp.where(lane >= s, pltpu.roll(x, s, axis=1), 0)
        s *= 2
    return x


def _route_kernel(topi_ref, slot_ref, tab_ref, meta_ref, cnt_ref, run_ref, pstart_ref, ltri_ref,
                  *, n_blocks_pad):
    p = pl.program_id(0)
    i = pl.program_id(1)
    lane = lax.broadcasted_iota(I32, (TM, LANES), 1)
    ti = topi_ref[...]
    e = [ti[:, k:k + 1] for k in range(TOP_K)]
    hot = jnp.zeros((TM, LANES), F32)
    for k in range(TOP_K):
        hot = hot + jnp.where(lane == e[k], 1.0, 0.0)
    colsum = jnp.sum(hot, axis=0, keepdims=True)

    @pl.when((p == 0) & (i == 0))
    def _():
        cnt_ref[...] = jnp.zeros_like(cnt_ref)
        r = lax.broadcasted_iota(I32, (TM, TM), 0)
        c = lax.broadcasted_iota(I32, (TM, TM), 1)
        ltri_ref[...] = jnp.where(r > c, 1.0, 0.0).astype(BF16)

    @pl.when(p == 0)
    def _():
        cnt_ref[...] += jnp.broadcast_to(colsum, cnt_ref.shape)

    @pl.when((p == 1) & (i == 0))
    def _():
        cnt = cnt_ref[...].astype(I32)
        nblk = _div(cnt + (BLK - 1), BLK)
        pend = _lane_cumsum(nblk)
        pstart = pend - nblk
        pstart_ref[...] = (pstart * BLK).astype(F32)
        run_ref[...] = jnp.zeros_like(run_ref)
        n_used = jnp.max(pend.astype(F32), axis=1, keepdims=True).astype(I32)
        brow = lax.broadcasted_iota(I32, (n_blocks_pad, LANES), 0)
        blane = lax.broadcasted_iota(I32, (n_blocks_pad, LANES), 1)
        done = jnp.where((pend[0:1] <= brow) & (blane < N_EXPERTS), 1.0, 0.0)
        block_e = jnp.minimum(jnp.sum(done, axis=1, keepdims=True).astype(I32), N_EXPERTS - 1)
        tab_ref[...] = jnp.broadcast_to(block_e, tab_ref.shape)
        mrow = lax.broadcasted_iota(I32, (SUBLANES, LANES), 0)
        meta = jnp.where(mrow == 0, cnt,
                         jnp.where(mrow == 1, pstart * BLK,
                                   jnp.where(mrow == 2, pend * BLK, n_used)))
        meta_ref[...] = meta

    @pl.when(p == 1)
    def _():
        before = _dot(ltri_ref[...], hot.astype(BF16))
        pos = pstart_ref[0:1] + run_ref[0:1] + before
        slot = jnp.zeros((TM, LANES), I32)
        for k in range(TOP_K):
            sk = jnp.sum(jnp.where(lane == e[k], pos, 0.0), axis=1, keepdims=True)
            slot = jnp.where(lane == k, sk.astype(I32), slot)
        slot_ref[...] = slot
        run_ref[...] += jnp.broadcast_to(colsum, run_ref.shape)


def _route(topi_all, n_blocks):
    n_total = topi_all.shape[0]
    n_blocks_pad = -(-n_blocks // SUBLANES) * SUBLANES
    stat = lambda: pltpu.VMEM((SUBLANES, LANES), F32)
    return pl.pallas_call(
        functools.partial(_route_kernel, n_blocks_pad=n_blocks_pad),
        grid=(2, n_total // TM),
        in_specs=[pl.BlockSpec((TM, LANES), lambda p, i: (i, 0))],
        out_specs=[
            pl.BlockSpec((TM, LANES), lambda p, i: (i * p, 0)),
            pl.BlockSpec((n_blocks_pad, LANES), lambda p, i: (0, 0)),
            pl.BlockSpec((SUBLANES, LANES), lambda p, i: (0, 0)),
        ],
        out_shape=[
            jax.ShapeDtypeStruct((n_total, LANES), I32),
            jax.ShapeDtypeStruct((n_blocks_pad, LANES), I32),
            jax.ShapeDtypeStruct((SUBLANES, LANES), I32),
        ],
        scratch_shapes=[stat(), stat(), stat(), pltpu.VMEM((TM, TM), BF16)],
        compiler_params=pltpu.CompilerParams(dimension_semantics=("arbitrary", "arbitrary")),
        name="route",
    )(topi_all)


TOKEN_BITS = 15


def _invert_kernel(slot_ref, meta_ref, inv_ref, *, n_tokens):
    n_assign = TOP_K * n_tokens

    def pack(row, tok):
        return lax.shift_left(jnp.asarray(row, I32), jnp.int32(TOKEN_BITS)) | jnp.asarray(tok, I32)

    @pl.when(pl.program_id(0) == 0)
    def _():
        def expert(e, seen):
            seen = seen + meta_ref[e]

            def pad(s, carry):
                inv_ref[s] = pack(n_assign + s - seen, 0)
                return carry
            lax.fori_loop(meta_ref[N_EXPERTS + e] + meta_ref[e], meta_ref[2 * N_EXPERTS + e], pad, 0)
            return seen
        lax.fori_loop(0, N_EXPERTS, expert, 0)

        def unused(s, carry):
            inv_ref[s] = pack(s, 0)
            return carry
        lax.fori_loop(meta_ref[3 * N_EXPERTS] * BLK, inv_ref.shape[0], unused, 0)

    tok0 = pl.program_id(0) * TM
    step = (1 << TOKEN_BITS) + 1
    for k in range(TOP_K):
        row0 = k * n_tokens + tok0

        def real(_, carry):
            row, val = carry
            for u in range(SUBLANES):
                inv_ref[slot_ref[row + u]] = val + u * step
            return row + SUBLANES, val + SUBLANES * step
        lax.fori_loop(0, TM // SUBLANES, real, (row0, pack(row0, tok0)))


def _invert(slot_kmajor, meta_flat, n_rows):
    n_tokens = slot_kmajor.shape[0] // TOP_K
    assert n_tokens <= 1 << TOKEN_BITS and n_rows <= 1 << (32 - TOKEN_BITS)
    return pl.pallas_call(
        functools.partial(_invert_kernel, n_tokens=n_tokens),
        grid_spec=pltpu.PrefetchScalarGridSpec(
            num_scalar_prefetch=2, grid=(n_tokens // TM,), in_specs=[],
            out_specs=pl.BlockSpec(memory_space=pltpu.SMEM)),
        out_shape=jax.ShapeDtypeStruct((n_rows,), I32),
        compiler_params=pltpu.CompilerParams(dimension_semantics=("arbitrary",)),
        name="invert",
    )(slot_kmajor, meta_flat)


GLU_GROUP = 2 * LANES


def _row_tile(ref, row):
    return ref.at[pl.ds(pl.multiple_of(row * ROW_TILES, ROW_TILES), ROW_TILES), :]


def _gather_rows(inv_ref, m_hbm, xbuf, sem, blk):
    for r in range(BLK):
        tok = inv_ref[blk * BLK + r] & ((1 << TOKEN_BITS) - 1)
        pltpu.make_async_copy(_row_tile(m_hbm, tok), _row_tile(xbuf, r), sem).start(priority=r % 2)


def _scatter_rows(inv_ref, ybuf, out_hbm, sem, blk):
    for r in range(BLK):
        row = lax.shift_right_logical(inv_ref[blk * BLK + r], TOKEN_BITS)
        pltpu.make_async_copy(_row_tile(ybuf, r), _row_tile(out_hbm, row), sem).start(priority=r % 2)


def _moe_kernel(inv_ref, be_ref, meta_ref, m_hbm, wup_hbm, bup_ref, wdn_hbm, bdn_ref, out_hbm,
                xbuf0, xbuf1, ybuf0, ybuf1, wup_st, wdn_st, wup_bf, wdn_bf, perm_ref, zbuf_ref,
                gsem, ssem, wsem, zsem):
    j = pl.program_id(0)
    n_used = meta_ref[3 * N_EXPERTS]
    xbufs, ybufs = (xbuf0, xbuf1), (ybuf0, ybuf1)

    def fetch_weights(e):
        pltpu.make_async_copy(wup_hbm.at[e], wup_st, wsem.at[0]).start()
        pltpu.make_async_copy(wdn_hbm.at[e], wdn_st, wsem.at[1]).start()

    def wait_gather(par):
        pltpu.make_async_copy(m_hbm.at[pl.ds(0, BLK * ROW_TILES), :], xbufs[par], gsem.at[par]).wait()

    def wait_scatter(par):
        pltpu.make_async_copy(ybufs[par], out_hbm.at[pl.ds(0, BLK * ROW_TILES), :], ssem.at[par]).wait()

    @pl.when(j == 0)
    def _():
        r = lax.broadcasted_iota(I32, (GLU_GROUP, GLU_GROUP), 0)
        c = lax.broadcasted_iota(I32, (GLU_GROUP, GLU_GROUP), 1)
        src = jnp.where(c < LANES, 2 * c, 2 * (c - LANES) + 1)
        perm_ref[...] = jnp.where(r == src, 1.0, 0.0).astype(BF16)
        zbuf_ref[...] = jnp.zeros_like(zbuf_ref)
        fetch_weights(be_ref[0])
        _gather_rows(inv_ref, m_hbm, xbuf0, gsem.at[0], 0)

    for par in range(2):
        b = 2 * j + par
        other = 1 - par
        active = b < n_used
        e = be_ref[b]

        @pl.when(active & (b >= 2))
        def _():
            wait_scatter(par)

        @pl.when(active)
        def _():
            wait_gather(par)
            _gather_rows(inv_ref, m_hbm, xbufs[other], gsem.at[other], b + 1)

        @pl.when(active & (b >= 1))
        def _():
            _scatter_rows(inv_ref, ybufs[other], out_hbm, ssem.at[other], b - 1)

        @pl.when(active & ((b == 0) | (e != be_ref[jnp.maximum(b - 1, 0)])))
        def _():
            pltpu.make_async_copy(wup_hbm.at[0], wup_st, wsem.at[0]).wait()
            pltpu.make_async_copy(wdn_hbm.at[0], wdn_st, wsem.at[1]).wait()
            for g in range(2 * D_FF // GLU_GROUP):
                wg = wup_st[:, g * GLU_GROUP:(g + 1) * GLU_GROUP].astype(BF16)
                pg = _dot(wg, perm_ref[...]).astype(BF16)
                wup_bf[:, g * LANES:(g + 1) * LANES] = pg[:, 0:LANES]
                wup_bf[:, D_FF + g * LANES:D_FF + (g + 1) * LANES] = pg[:, LANES:]
            wdn_bf[...] = wdn_st[...].astype(BF16)
            next_first = _div(meta_ref[2 * N_EXPERTS + e], BLK)

            @pl.when(next_first < n_used)
            def _():
                fetch_weights(be_ref[next_first])

        @pl.when(active)
        def _():
            h = _dot(_load_row_tiles(xbufs[par], BF16), wup_bf[...]) + bup_ref[pl.ds(e, 1), :]
            glu = jnp.minimum(h[:, 0:D_FF], SWIGLU_LIMIT)
            lin = jnp.clip(h[:, D_FF:], -SWIGLU_LIMIT, SWIGLU_LIMIT)
            act = glu * jax.nn.sigmoid(SWIGLU_ALPHA * glu) * (lin + 1.0)
            _store_row_tiles(ybufs[par], _dot(act.astype(BF16), wdn_bf[...]) + bdn_ref[pl.ds(e, 1), :])

        @pl.when(b == n_used)
        def _():
            wait_gather(par)
            _scatter_rows(inv_ref, ybufs[other], out_hbm, ssem.at[other], b - 1)
            wait_scatter(other)

            @pl.when(b >= 2)
            def _():
                wait_scatter(par)

        @pl.when(b >= n_used)
        def _():
            half = zbuf_ref.shape[0]
            for part in range(BLK * ROW_TILES // half):
                start = pl.multiple_of(b * (BLK * ROW_TILES) + part * half, half)
                cp = pltpu.make_async_copy(zbuf_ref, out_hbm.at[pl.ds(start, half), :], zsem)
                cp.start()
                cp.wait()


def _moe(inv, block_e, meta_flat, m_all, wup, bup, wdn, bdn):
    n_rows = inv.shape[0]
    any_spec = pl.BlockSpec(memory_space=pl.ANY)
    xy = lambda: pltpu.VMEM((BLK * ROW_TILES, LANES), F32)
    return pl.pallas_call(
        _moe_kernel,
        grid_spec=pltpu.PrefetchScalarGridSpec(
            num_scalar_prefetch=3,
            grid=(n_rows // (2 * BLK),),
            in_specs=[
                any_spec, any_spec,
                pl.BlockSpec((N_EXPERTS, 2 * D_FF), lambda j, *_: (0, 0)),
                any_spec,
                pl.BlockSpec((N_EXPERTS, D), lambda j, *_: (0, 0)),
            ],
            out_specs=any_spec,
            scratch_shapes=[
                xy(), xy(), xy(), xy(),
                pltpu.VMEM((D, 2 * D_FF), F32), pltpu.VMEM((D_FF, D), F32),
                pltpu.VMEM((D, 2 * D_FF), BF16), pltpu.VMEM((D_FF, D), BF16),
                pltpu.VMEM((GLU_GROUP, GLU_GROUP), BF16),
                pltpu.VMEM((BLK * ROW_TILES // 2, LANES), F32),
                pltpu.SemaphoreType.DMA((2,)), pltpu.SemaphoreType.DMA((2,)),
                pltpu.SemaphoreType.DMA((2,)), pltpu.SemaphoreType.DMA,
            ],
        ),
        out_shape=jax.ShapeDtypeStruct((n_rows * ROW_TILES, LANES), F32),
        compiler_params=pltpu.CompilerParams(
            dimension_semantics=("arbitrary",), vmem_limit_bytes=VMEM_LIMIT),
        name="moe",
    )(inv, block_e, meta_flat, m_all, wup, bup, wdn, bdn)


def _tail_kernel(ye0_ref, ye1_ref, ye2_ref, ye3_ref, gate_ref, h1_ref, pp_ref, pd_ref, gple_ref,
                 wpg_ref, wple_ref, gfin_ref, yp_ref, yd_ref, *, prompt_tiles):
    i = pl.program_id(0)
    gate = gate_ref[...]
    moe = None
    for k, ye_ref in enumerate((ye0_ref, ye1_ref, ye2_ref, ye3_ref)):
        term = gate[:, k:k + 1] * _load_row_tiles(ye_ref)
        moe = term if moe is None else moe + term
    h2 = h1_ref[...] + moe
    pg = jax.nn.sigmoid(_dot(_rms(h2, gple_ref[...]).astype(BF16), wpg_ref[...]))
    p = jnp.where(i < prompt_tiles, pp_ref[...], pd_ref[...])
    h3 = h2 + pg * _dot(p.astype(BF16), wple_ref[...])
    y = _rms(h3, gfin_ref[...])

    @pl.when(i < prompt_tiles)
    def _():
        yp_ref[...] = y

    @pl.when(i >= prompt_tiles)
    def _():
        yd_ref[...] = y


def _tail(y_exp, gate_all, h1_all, p_prompt, p_dec, gple, wpg, wple, gfin):
    n_total = h1_all.shape[0]
    n_prompt = p_prompt.shape[0]
    prompt_tiles = n_prompt // TMT
    tiles = n_total // TMT
    row_spec = lambda w: pl.BlockSpec((TMT, w), lambda i: (i, 0))
    plane = lambda k: pl.BlockSpec((TMT * ROW_TILES, LANES), lambda i: (k * tiles + i, 0))
    prompt_rows = lambda w: pl.BlockSpec((TMT, w), lambda i: (jnp.minimum(i, prompt_tiles - 1), 0))
    dec_rows = lambda w: pl.BlockSpec((TMT, w), lambda i: (jnp.maximum(i - prompt_tiles, 0), 0))
    return pl.pallas_call(
        functools.partial(_tail_kernel, prompt_tiles=prompt_tiles),
        grid=(tiles,),
        in_specs=[
            plane(0), plane(1), plane(2), plane(3), row_spec(LANES), row_spec(D),
            prompt_rows(PLE_DIM), dec_rows(PLE_DIM),
            _const_spec((1, D)), _const_spec((D, D)), _const_spec((PLE_DIM, D)), _const_spec((1, D)),
        ],
        out_specs=[prompt_rows(D), dec_rows(D)],
        out_shape=[jax.ShapeDtypeStruct((n_prompt, D), F32),
                   jax.ShapeDtypeStruct((n_total - n_prompt, D), F32)],
        compiler_params=pltpu.CompilerParams(
            dimension_semantics=("arbitrary",), vmem_limit_bytes=VMEM_LIMIT),
        name="tail",
    )(y_exp, y_exp, y_exp, y_exp, gate_all, h1_all, p_prompt, p_dec, gple, wpg, wple, gfin)


def _widen_heads(w_q):
    wq = w_q.reshape(D, N_HEADS, HEAD_DIM)
    z = jnp.zeros_like(wq)
    lo = jnp.concatenate([wq, z], axis=-1)
    hi = jnp.concatenate([z, wq], axis=-1)
    kv_of_head = (jnp.arange(N_HEADS) // GROUP)[None, :, None]
    return jnp.where(kv_of_head == 0, lo, hi).reshape(D, N_HEADS * LANES)


def kernel(x_prompt, x_sample, state_conv, cache_k, cache_v, p_prompt, p_sample, g_mix, w_in, conv_w,
           sinks, w_branch, w_out, g_ffn, w_router, b_router, w_up, b_up, w_down, b_down, g_ple,
           w_ple_gate, w_ple, g_final):
    depth = w_in.shape[0]
    assert depth == 1, "single-layer step"
    n_batch, seq, _ = x_prompt.shape
    n_seq, n_new, _ = x_sample.shape
    wb = cache_k.shape[2]
    assert wb == PAST_WB == WINDOW and seq % TM == 0 and n_seq * n_new == TM
    n_prompt = n_batch * seq
    n_dec = n_seq * n_new
    n_total = n_prompt + n_dec
    n_blocks = -(-(n_total * TOP_K) // BLK) + N_EXPERTS
    assert n_blocks % 2 == 0
    n_rows = n_blocks * BLK

    win = w_in[0].astype(BF16)
    w_q = w_in[0][:, C_Q:C_KV]
    win_s = jnp.concatenate([w_in[0][:, :C_Q], _widen_heads(w_q), w_in[0][:, C_KV:]], axis=1).astype(BF16)
    wbr = w_branch[0].astype(BF16)
    wbr1e = jnp.transpose(_widen_heads(jnp.transpose(w_branch[0, 1]))).astype(BF16)
    wout = w_out[0].astype(BF16)
    wr_f = jnp.pad(w_router[0], ((0, 0), (0, LANES - N_EXPERTS)))
    wr_hi = wr_f.astype(BF16)
    wr = jnp.concatenate([wr_hi, (wr_f - wr_hi.astype(F32)).astype(BF16)], axis=1)
    br = jnp.pad(b_router[0], (0, LANES - N_EXPERTS), constant_values=-1e30).reshape(1, LANES)
    gmix = g_mix[0].reshape(1, D)
    gffn = g_ffn[0].reshape(1, D)
    bu = b_up[0].reshape(N_EXPERTS, D_FF, 2)
    bup = jnp.concatenate([bu[..., 0], bu[..., 1]], axis=-1)
    bdn = b_down[0]

    xp = x_prompt.reshape(n_prompt, D)
    h1_all, m_all, topi_all, gate_all, conv_p, k_p, v_p = _mixer_prompt(
        xp, gmix, win, conv_w[0], sinks[0], wbr, wout, gffn, wr, br, n_total, seq)
    xs_t = jnp.transpose(x_sample, (1, 0, 2)).reshape(n_dec, D)
    state_t = jnp.transpose(state_conv[0], (1, 0, 2))
    ck = cache_k[0].reshape(n_seq * wb, KV_W)
    cv = cache_v[0].reshape(n_seq * wb, KV_W)
    h1_all, m_all, topi_all, gate_all, conv_s, k_s, v_s = _mixer_sample(
        xs_t, gmix, win_s, conv_w[0], state_t, sinks[0], ck, cv, wbr[0], wbr1e, wout, gffn, wr, br,
        h1_all, m_all, topi_all, gate_all, n_seq, n_new)

    slot_w, tab, meta = _route(topi_all, n_blocks)
    slot_kmajor = jnp.transpose(slot_w[:, :TOP_K]).reshape(-1)
    meta_flat = jnp.concatenate([meta[0:3, :N_EXPERTS].reshape(-1), meta[3, 0:1]])
    inv = _invert(slot_kmajor, meta_flat, n_rows)
    y_exp = _moe(inv, tab[:n_blocks, 0], meta_flat, m_all, w_up[0], bup, w_down[0], bdn)

    y_p, y_d = _tail(y_exp, gate_all, h1_all, p_prompt[0].reshape(n_prompt, PLE_DIM),
                     jnp.transpose(p_sample[0], (1, 0, 2)).reshape(n_dec, PLE_DIM),
                     g_ple[0].reshape(1, D), w_ple_gate[0].astype(BF16), w_ple[0].astype(BF16),
                     g_final.reshape(1, D))

    y_prompt = y_p.reshape(n_batch, seq, D)
    y_sample = jnp.transpose(y_d.reshape(n_new, n_seq, D), (1, 0, 2))
    conv_prompt = conv_p[:, SUBLANES - 2:, :][None]
    conv_sample = jnp.transpose(conv_s, (1, 0, 2))[None]
    kv_shape = (1, -1, wb, N_KV, HEAD_DIM)
    return (y_prompt, y_sample, conv_prompt, conv_sample,
            k_p.reshape(kv_shape), v_p.reshape(kv_shape), k_s.reshape(kv_shape), v_s.reshape(kv_shape))
```

```python
import functools

import jax
import jax.numpy as jnp
from jax import lax
from jax.experimental import pallas as pl
from jax.experimental.pallas import tpu as pltpu

F32 = jnp.float32
BF16 = jnp.bfloat16
I32 = jnp.int32

D = 1024
CONV_W = 512
HEAD_DIM = 64
N_HEADS = 8
N_KV = 2
GROUP = N_HEADS // N_KV
ATT_W = N_HEADS * HEAD_DIM
KV_W = N_KV * HEAD_DIM
WINDOW = 128
ATT_SCALE = HEAD_DIM ** -0.5
N_EXPERTS = 32
TOP_K = 4
D_FF = 1024
PLE_DIM = 256
SWIGLU_ALPHA = 1.702
SWIGLU_LIMIT = 7.0
RMS_EPS = 1e-5
PAST_WB = 128

LANES = 128
SUBLANES = 8
VMEM_LIMIT = 56 * 1024 * 1024

TM = 512
TMT = 256
BLK = 256
SEQ_GROUP = 8

C_Q = 3 * CONV_W
C_KV = C_Q + ATT_W
C_GL = C_KV + 2 * KV_W
IN_W = C_GL + 2 * D
CS_KV = C_Q + N_HEADS * LANES
CS_GL = CS_KV + 2 * KV_W
IN_WS = CS_GL + 2 * D

NEG_INF = float("-inf")


def _slope(head):
    return 2.0 ** (-(head + 1))


def _rms(x, g):
    r = lax.rsqrt(jnp.mean(x * x, axis=-1, keepdims=True) + RMS_EPS)
    return x * r * g


def _div(x, n):
    assert n & (n - 1) == 0
    return lax.shift_right_logical(x, n.bit_length() - 1)


def _mod(x, n):
    assert n & (n - 1) == 0
    return x & (n - 1)


ROW_TILES = D // LANES


def _store_row_tiles(ref, x):
    n = x.shape[0]
    for c in range(ROW_TILES):
        ref[pl.ds(c, n, stride=ROW_TILES), :] = x[:, c * LANES:(c + 1) * LANES]


def _load_row_tiles(ref, dtype=F32):
    n = ref.shape[0] // ROW_TILES
    return jnp.concatenate(
        [ref[pl.ds(c, n, stride=ROW_TILES), :].astype(dtype) for c in range(ROW_TILES)], axis=1)


def _dot(a, b):
    return jnp.dot(a, b, preferred_element_type=F32)


def _dot_nt(a, b):
    return lax.dot_general(a, b, (((1,), (1,)), ((), ())), preferred_element_type=F32)


def _router(m, wr_ref, br_ref):
    m_hi = m.astype(BF16)
    m_lo = (m - m_hi.astype(F32)).astype(BF16)
    hi_part = _dot(m_hi, wr_ref[...])
    logits = (hi_part[:, 0:LANES] + hi_part[:, LANES:] + _dot(m_lo, wr_ref[:, 0:LANES])) + br_ref[...]
    tm = logits.shape[0]
    lane = lax.broadcasted_iota(I32, (tm, LANES), 1)
    lanef = lane.astype(F32)
    vals, idxs = [], []
    l = logits
    for _ in range(TOP_K):
        mx = jnp.max(l, axis=-1, keepdims=True)
        idx = jnp.min(jnp.where(l == mx, lanef, float(LANES)), axis=-1, keepdims=True)
        vals.append(mx)
        idxs.append(idx.astype(I32))
        l = jnp.where(lanef == idx, NEG_INF, l)
    es = [jnp.exp(v - vals[0]) for v in vals]
    den = es[0] + es[1] + es[2] + es[3]
    topi = jnp.zeros((tm, LANES), I32)
    gate = jnp.zeros((tm, LANES), F32)
    for k in range(TOP_K):
        topi = jnp.where(lane == k, idxs[k], topi)
        gate = jnp.where(lane == k, es[k] / den, gate)
    return topi, gate


def _merge_out(x, a_bf, yc_bf, ya_bf, win_gl, w_br0, w_br1, wout_ref):
    zg = _dot(a_bf, win_gl)
    yb0 = _dot(yc_bf, w_br0)
    yb1 = _dot(ya_bf, w_br1)
    merged = jax.nn.sigmoid(zg[:, :D]) * yb0 + jax.nn.sigmoid(zg[:, D:]) * yb1
    return x + _dot(merged.astype(BF16), wout_ref[...])


def _mixer_prompt_kernel(*refs, tiles_per_seq, n_tiles):
    i = pl.program_id(0)

    @pl.when(i < n_tiles)
    def _():
        _mixer_prompt_tile(*refs, tiles_per_seq=tiles_per_seq)

    @pl.when(i == n_tiles)
    def _():
        for ref in refs[10:14]:
            ref[...] = jnp.zeros_like(ref)


def _mixer_prompt_tile(x_ref, gmix_ref, win_ref, convw_ref, sinks_ref, wbr_ref, wout_ref,
                       gffn_ref, wr_ref, br_ref,
                       h1_ref, m_ref, topi_ref, gate_ref, conv_ref, kst_ref, vst_ref,
                       ucarry_ref, kd_ref, vd_ref, ya_ref, bias_ref, *, tiles_per_seq):
    ti = pl.program_id(0) % tiles_per_seq

    @pl.when(pl.program_id(0) == 0)
    def _():
        qi = _mod(lax.broadcasted_iota(I32, (GROUP * WINDOW, 2 * WINDOW), 0), WINDOW)
        g = _div(lax.broadcasted_iota(I32, (GROUP * WINDOW, 2 * WINDOW), 0), WINDOW)
        dist = qi - lax.broadcasted_iota(I32, (GROUP * WINDOW, 2 * WINDOW), 1) + WINDOW
        distf = dist.astype(F32)
        for h2 in range(N_KV):
            slope = jnp.zeros(dist.shape, F32)
            for i in range(GROUP):
                slope = jnp.where(g == i, _slope(h2 * GROUP + i), slope)
            bias_ref[h2] = jnp.where((dist >= 0) & (dist < WINDOW), -(slope * distf), NEG_INF)

    @pl.when(ti == 0)
    def _():
        ucarry_ref[...] = jnp.zeros_like(ucarry_ref)
        kd_ref[:, 0:WINDOW, :] = jnp.zeros((N_KV, WINDOW, LANES), BF16)
        vd_ref[:, 0:WINDOW, :] = jnp.zeros((N_KV, WINDOW, LANES), BF16)

    x = x_ref[...]
    a = _rms(x, gmix_ref[...]).astype(BF16)

    zc = _dot(a, win_ref[:, 0:C_Q])
    u = zc[:, CONV_W:2 * CONV_W] * zc[:, 2 * CONV_W:3 * CONV_W]
    row = lax.broadcasted_iota(I32, (TM, CONV_W), 0)
    c0 = ucarry_ref[SUBLANES - 2:SUBLANES - 1, :]
    c1 = ucarry_ref[SUBLANES - 1:SUBLANES, :]
    u1 = jnp.where(row == 0, c1, pltpu.roll(u, 1, axis=0))
    u2 = jnp.where(row == 0, c0, jnp.where(row == 1, c1, pltpu.roll(u, 2, axis=0)))
    cw = convw_ref[...]
    yc = zc[:, 0:CONV_W] * (cw[0:1] * u2 + cw[1:2] * u1 + cw[2:3] * u)
    ucarry_ref[...] = u[TM - SUBLANES:, :]
    conv_ref[0] = u[TM - SUBLANES:, :]

    zkv = _dot(a, win_ref[:, C_KV:C_GL])
    k = zkv[:, 0:KV_W]
    v = zkv[:, KV_W:]
    kst_ref[0] = k[TM - WINDOW:, :]
    vst_ref[0] = v[TM - WINDOW:, :]
    lo = lax.broadcasted_iota(I32, (TM, LANES), 1) < HEAD_DIM
    for src, dst in ((k, kd_ref), (v, vd_ref)):
        s_lo = jnp.where(lo, src, 0.0)
        s_hi = jnp.where(lo, 0.0, src)
        dst[0, WINDOW:, :] = (s_lo + pltpu.roll(s_lo, HEAD_DIM, axis=1)).astype(BF16)
        dst[1, WINDOW:, :] = (s_hi + pltpu.roll(s_hi, HEAD_DIM, axis=1)).astype(BF16)

    qf = _dot(a, win_ref[:, C_Q:C_KV]) * ATT_SCALE
    lo_all = (lax.broadcasted_iota(I32, (TM, ATT_W), 1) & (LANES - 1)) < HEAD_DIM
    q_lo = jnp.where(lo_all, qf, 0.0).astype(BF16)
    q_hi = jnp.where(lo_all, 0.0, qf).astype(BF16)
    lo_q = lax.broadcasted_iota(I32, (WINDOW, LANES), 1) < HEAD_DIM
    first_keys_ok = (lax.broadcasted_iota(I32, (1, 2 * WINDOW), 1) + ti * TM) >= WINDOW
    hrow = _div(lax.broadcasted_iota(I32, (GROUP * WINDOW, 1), 0), WINDOW)
    for h2 in range(N_KV):
        sink = jnp.zeros((GROUP * WINDOW, 1), F32)
        for g in range(GROUP):
            sink = jnp.where(hrow == g, sinks_ref[h2 * GROUP + g], sink)
        for b in range(TM // WINDOW):
            rb = b * WINDOW
            cols = [(h2 * (GROUP // 2) + pr) * LANES for pr in range(GROUP // 2)]
            qs = jnp.concatenate(
                [part[rb:rb + WINDOW, c:c + LANES] for c in cols for part in (q_lo, q_hi)], axis=0)
            s = _dot_nt(qs, kd_ref[h2, rb:rb + 2 * WINDOW, :]) + bias_ref[h2]
            if b == 0:
                s = jnp.where(first_keys_ok, s, NEG_INF)
            mrow = jnp.maximum(jnp.max(s, axis=-1, keepdims=True), sink)
            p = jnp.exp(s - mrow)
            den = jnp.sum(p, axis=-1, keepdims=True) + jnp.exp(sink - mrow)
            o = _dot(p.astype(BF16), vd_ref[h2, rb:rb + 2 * WINDOW, :]) * (1.0 / den)
            for pr, c in enumerate(cols):
                r0 = 2 * pr * WINDOW
                ya_ref[rb:rb + WINDOW, c:c + LANES] = jnp.where(
                    lo_q, o[r0:r0 + WINDOW], o[r0 + WINDOW:r0 + 2 * WINDOW]).astype(BF16)
    kd_ref[:, 0:WINDOW, :] = kd_ref[:, TM:TM + WINDOW, :]
    vd_ref[:, 0:WINDOW, :] = vd_ref[:, TM:TM + WINDOW, :]

    h1 = _merge_out(x, a, yc.astype(BF16), ya_ref[...], win_ref[:, C_GL:IN_W],
                    wbr_ref[0], wbr_ref[1], wout_ref)
    h1_ref[...] = h1
    m = _rms(h1, gffn_ref[...])
    _store_row_tiles(m_ref, m)
    topi, gate = _router(m, wr_ref, br_ref)
    topi_ref[...] = topi
    gate_ref[...] = gate


def _const_spec(shape):
    return pl.BlockSpec(shape, lambda *_: (0,) * len(shape))


def _mixer_prompt(xp, gmix, win, convw, sinks, wbr, wout, gffn, wr, br, n_total, seq):
    n_prompt = xp.shape[0]
    n_batch = n_prompt // seq
    tiles_per_seq = seq // TM
    n_tiles = n_prompt // TM
    assert n_total == n_prompt + TM
    row_spec = lambda w: pl.BlockSpec((TM, w), lambda i: (i, 0))
    state_spec = lambda r: pl.BlockSpec(
        (1, r, LANES if r == WINDOW else CONV_W),
        lambda i: (jnp.minimum(i // tiles_per_seq, n_batch - 1), 0, 0))
    return pl.pallas_call(
        functools.partial(_mixer_prompt_kernel, tiles_per_seq=tiles_per_seq, n_tiles=n_tiles),
        grid=(n_tiles + 1,),
        in_specs=[
            pl.BlockSpec((TM, D), lambda i: (jnp.minimum(i, n_tiles - 1), 0)),
            _const_spec((1, D)), _const_spec((D, IN_W)), _const_spec((3, CONV_W)),
            pl.BlockSpec(memory_space=pltpu.SMEM),
            _const_spec((2, CONV_W, D)), _const_spec((D, D)), _const_spec((1, D)),
            _const_spec((D, 2 * LANES)), _const_spec((1, LANES)),
        ],
        out_specs=[
            row_spec(D), pl.BlockSpec((TM * ROW_TILES, LANES), lambda i: (i, 0)),
            row_spec(LANES), row_spec(LANES),
            state_spec(SUBLANES), state_spec(WINDOW), state_spec(WINDOW),
        ],
        out_shape=[
            jax.ShapeDtypeStruct((n_total, D), F32),
            jax.ShapeDtypeStruct((n_total * ROW_TILES, LANES), F32),
            jax.ShapeDtypeStruct((n_total, LANES), I32), jax.ShapeDtypeStruct((n_total, LANES), F32),
            jax.ShapeDtypeStruct((n_batch, SUBLANES, CONV_W), F32),
            jax.ShapeDtypeStruct((n_batch, WINDOW, KV_W), F32),
            jax.ShapeDtypeStruct((n_batch, WINDOW, KV_W), F32),
        ],
        scratch_shapes=[
            pltpu.VMEM((SUBLANES, CONV_W), F32),
            pltpu.VMEM((N_KV, TM + WINDOW, LANES), BF16),
            pltpu.VMEM((N_KV, TM + WINDOW, LANES), BF16),
            pltpu.VMEM((TM, ATT_W), BF16),
            pltpu.VMEM((N_KV, GROUP * WINDOW, 2 * WINDOW), F32),
        ],
        compiler_params=pltpu.CompilerParams(
            dimension_semantics=("arbitrary",), vmem_limit_bytes=VMEM_LIMIT),
        name="mixer_prompt",
    )(xp, gmix, win, convw, sinks, wbr, wout, gffn, wr, br)


def _mixer_sample_kernel(x_ref, gmix_ref, win_ref, convw_ref, st_ref, sinks_ref, ck_ref, cv_ref,
                         wbr0_ref, wbr1_ref, wout_ref, gffn_ref, wr_ref, br_ref,
                         h1_in, m_in, topi_in, gate_in,
                         h1_ref, m_ref, topi_ref, gate_ref, conv_ref, kout_ref, vout_ref,
                         a_ref, yc_ref, q_ref, knew_ref, vnew_ref, ya_ref, bias_ref,
                         *, n_seq, n_new):
    del h1_in, m_in, topi_in, gate_in
    g = pl.program_id(0)
    rows_c = SEQ_GROUP * PAST_WB
    rows_n = SEQ_GROUP * n_new
    q_rows = GROUP * rows_n

    @pl.when(g == 0)
    def _():
        x = x_ref[...]
        a = _rms(x, gmix_ref[...]).astype(BF16)
        a_ref[...] = a
        zc = _dot(a, win_ref[:, 0:C_Q])
        u = zc[:, CONV_W:2 * CONV_W] * zc[:, 2 * CONV_W:3 * CONV_W]
        up = [st_ref[0], st_ref[1]] + [u[t * n_seq:(t + 1) * n_seq] for t in range(n_new)]
        cw = convw_ref[...]
        for t in range(n_new):
            y = zc[t * n_seq:(t + 1) * n_seq, 0:CONV_W] * (
                cw[0:1] * up[t] + cw[1:2] * up[t + 1] + cw[2:3] * up[t + 2])
            yc_ref[t * n_seq:(t + 1) * n_seq, :] = y.astype(BF16)
        conv_ref[0] = up[n_new]
        conv_ref[1] = up[n_new + 1]
        q_ref[...] = _dot(a, win_ref[:, C_Q:CS_KV]) * ATT_SCALE
        zkv = _dot(a, win_ref[:, CS_KV:CS_GL])
        knew_ref[...] = zkv[:, 0:KV_W]
        vnew_ref[...] = zkv[:, KV_W:]
        r = lax.broadcasted_iota(I32, (q_rows, rows_c + rows_n), 0)
        c = lax.broadcasted_iota(I32, (q_rows, rows_c + rows_n), 1)
        hl = _div(r, rows_n)
        t = _div(_mod(r, rows_n), SEQ_GROUP)
        j = _mod(r, SEQ_GROUP)
        is_cache = c < rows_c
        cn = jnp.maximum(c - rows_c, 0)
        kseq = jnp.where(is_cache, _div(c, PAST_WB), _mod(cn, SEQ_GROUP))
        dist = jnp.where(is_cache, PAST_WB + t - _mod(c, PAST_WB), t - _div(cn, SEQ_GROUP))
        valid = (kseq == j) & (dist >= 0) & (dist < WINDOW)
        distf = dist.astype(F32)
        for h2 in range(N_KV):
            slope = jnp.zeros(r.shape, F32)
            for i in range(GROUP):
                slope = jnp.where(hl == i, _slope(h2 * GROUP + i), slope)
            bias_ref[h2] = jnp.where(valid, -(slope * distf), NEG_INF)

    base = pl.multiple_of(g * SEQ_GROUP, SEQ_GROUP)
    kc = ck_ref[...]
    vc = cv_ref[...]
    new_rows = lambda ref: [ref[pl.ds(t * n_seq + base, SEQ_GROUP), :] for t in range(n_new)]
    kgrp = jnp.concatenate([kc] + new_rows(knew_ref), axis=0).astype(BF16)
    vgrp = jnp.concatenate([vc] + new_rows(vnew_ref), axis=0).astype(BF16)
    hl_row = _div(lax.broadcasted_iota(I32, (q_rows, 1), 0), rows_n)
    for h2 in range(N_KV):
        qs = jnp.concatenate(
            [q_ref[pl.ds(t * n_seq + base, SEQ_GROUP), (h2 * GROUP + i) * LANES:(h2 * GROUP + i + 1) * LANES]
             for i in range(GROUP) for t in range(n_new)], axis=0).astype(BF16)
        s = _dot_nt(qs, kgrp) + bias_ref[h2]
        sink = jnp.zeros((q_rows, 1), F32)
        for i in range(GROUP):
            sink = jnp.where(hl_row == i, sinks_ref[h2 * GROUP + i], sink)
        mrow = jnp.maximum(jnp.max(s, axis=-1, keepdims=True), sink)
        p = jnp.exp(s - mrow)
        den = jnp.sum(p, axis=-1, keepdims=True) + jnp.exp(sink - mrow)
        o = _dot(p.astype(BF16), vgrp) * (1.0 / den)
        for i in range(GROUP):
            for t in range(n_new):
                r0 = (i * n_new + t) * SEQ_GROUP
                ya_ref[pl.ds(t * n_seq + base, SEQ_GROUP),
                       (h2 * GROUP + i) * LANES:(h2 * GROUP + i + 1) * LANES] = o[r0:r0 + SEQ_GROUP]

    for cache, new_ref, out in ((kc, knew_ref, kout_ref), (vc, vnew_ref, vout_ref)):
        out[...] = pltpu.roll(cache, rows_c - n_new, axis=0)
        for j in range(SEQ_GROUP):
            for t in range(n_new):
                out[pl.ds(j * PAST_WB + PAST_WB - n_new + t, 1), :] = (
                    new_ref[pl.ds(t * n_seq + base + j, 1), :])

    @pl.when(g == pl.num_programs(0) - 1)
    def _():
        h1 = _merge_out(x_ref[...], a_ref[...], yc_ref[...], ya_ref[...].astype(BF16),
                        win_ref[:, CS_GL:IN_WS], wbr0_ref[...], wbr1_ref[...], wout_ref)
        h1_ref[...] = h1
        m = _rms(h1, gffn_ref[...])
        _store_row_tiles(m_ref, m)
        topi, gate = _router(m, wr_ref, br_ref)
        topi_ref[...] = topi
        gate_ref[...] = gate


def _mixer_sample(xs, gmix, win_s, convw, state, sinks, ck, cv, wbr0, wbr1e, wout, gffn, wr, br,
                  h1_all, m_all, topi_all, gate_all, n_seq, n_new):
    n_tok = xs.shape[0]
    n_total = h1_all.shape[0]
    blk0 = (n_total - n_tok) // n_tok
    rows_c = SEQ_GROUP * PAST_WB
    rows_n = SEQ_GROUP * n_new
    any_spec = pl.BlockSpec(memory_space=pl.ANY)
    tok_spec = lambda w: pl.BlockSpec((n_tok, w), lambda g: (blk0, 0))
    cache_spec = pl.BlockSpec((rows_c, KV_W), lambda g: (g, 0))
    return pl.pallas_call(
        functools.partial(_mixer_sample_kernel, n_seq=n_seq, n_new=n_new),
        grid=(n_seq // SEQ_GROUP,),
        in_specs=[
            _const_spec((n_tok, D)), _const_spec((1, D)), _const_spec((D, IN_WS)),
            _const_spec((3, CONV_W)), _const_spec((2, n_seq, CONV_W)),
            pl.BlockSpec(memory_space=pltpu.SMEM), cache_spec, cache_spec,
            _const_spec((CONV_W, D)), _const_spec((N_HEADS * LANES, D)), _const_spec((D, D)),
            _const_spec((1, D)), _const_spec((D, 2 * LANES)), _const_spec((1, LANES)),
            any_spec, any_spec, any_spec, any_spec,
        ],
        out_specs=[
            tok_spec(D), pl.BlockSpec((n_tok * ROW_TILES, LANES), lambda g: (blk0, 0)),
            tok_spec(LANES), tok_spec(LANES),
            _const_spec((2, n_seq, CONV_W)), cache_spec, cache_spec,
        ],
        out_shape=[
            jax.ShapeDtypeStruct(h1_all.shape, F32), jax.ShapeDtypeStruct(m_all.shape, F32),
            jax.ShapeDtypeStruct(topi_all.shape, I32), jax.ShapeDtypeStruct(gate_all.shape, F32),
            jax.ShapeDtypeStruct((2, n_seq, CONV_W), F32),
            jax.ShapeDtypeStruct(ck.shape, F32), jax.ShapeDtypeStruct(cv.shape, F32),
        ],
        input_output_aliases={14: 0, 15: 1, 16: 2, 17: 3},
        scratch_shapes=[
            pltpu.VMEM((n_tok, D), BF16),
            pltpu.VMEM((n_tok, CONV_W), BF16),
            pltpu.VMEM((n_tok, N_HEADS * LANES), F32),
            pltpu.VMEM((n_tok, KV_W), F32), pltpu.VMEM((n_tok, KV_W), F32),
            pltpu.VMEM((n_tok, N_HEADS * LANES), F32),
            pltpu.VMEM((N_KV, GROUP * rows_n, rows_c + rows_n), F32),
        ],
        compiler_params=pltpu.CompilerParams(
            dimension_semantics=("arbitrary",), vmem_limit_bytes=VMEM_LIMIT),
        name="mixer_sample",
    )(xs, gmix, win_s, convw, state, sinks, ck, cv, wbr0, wbr1e, wout, gffn, wr, br,
      h1_all, m_all, topi_all, gate_all)


def _lane_cumsum(x):
    lane = lax.broadcasted_iota(I32, x.shape, 1)
    s = 1
    while s < LANES:
        x = x + jnp.where(lane >= s, pltpu.roll(x, s, axis=1), 0)
        s *= 2
    return x


def _route_kernel(topi_ref, slot_ref, tab_ref, meta_ref, cnt_ref, run_ref, pstart_ref, ltri_ref,
                  *, n_blocks_pad):
    p = pl.program_id(0)
    i = pl.program_id(1)
    lane = lax.broadcasted_iota(I32, (TM, LANES), 1)
    ti = topi_ref[...]
    e = [ti[:, k:k + 1] for k in range(TOP_K)]
    hot = jnp.zeros((TM, LANES), F32)
    for k in range(TOP_K):
        hot = hot + jnp.where(lane == e[k], 1.0, 0.0)
    colsum = jnp.sum(hot, axis=0, keepdims=True)

    @pl.when((p == 0) & (i == 0))
    def _():
        cnt_ref[...] = jnp.zeros_like(cnt_ref)
        r = lax.broadcasted_iota(I32, (TM, TM), 0)
        c = lax.broadcasted_iota(I32, (TM, TM), 1)
        ltri_ref[...] = jnp.where(r > c, 1.0, 0.0).astype(BF16)

    @pl.when(p == 0)
    def _():
        cnt_ref[...] += jnp.broadcast_to(colsum, cnt_ref.shape)

    @pl.when((p == 1) & (i == 0))
    def _():
        cnt = cnt_ref[...].astype(I32)
        nblk = _div(cnt + (BLK - 1), BLK)
        pend = _lane_cumsum(nblk)
        pstart = pend - nblk
        pstart_ref[...] = (pstart * BLK).astype(F32)
        run_ref[...] = jnp.zeros_like(run_ref)
        n_used = jnp.max(pend.astype(F32), axis=1, keepdims=True).astype(I32)
        brow = lax.broadcasted_iota(I32, (n_blocks_pad, LANES), 0)
        blane = lax.broadcasted_iota(I32, (n_blocks_pad, LANES), 1)
        done = jnp.where((pend[0:1] <= brow) & (blane < N_EXPERTS), 1.0, 0.0)
        block_e = jnp.minimum(jnp.sum(done, axis=1, keepdims=True).astype(I32), N_EXPERTS - 1)
        tab_ref[...] = jnp.broadcast_to(block_e, tab_ref.shape)
        mrow = lax.broadcasted_iota(I32, (SUBLANES, LANES), 0)
        meta = jnp.where(mrow == 0, cnt,
                         jnp.where(mrow == 1, pstart * BLK,
                                   jnp.where(mrow == 2, pend * BLK, n_used)))
        meta_ref[...] = meta

    @pl.when(p == 1)
    def _():
        before = _dot(ltri_ref[...], hot.astype(BF16))
        pos = pstart_ref[0:1] + run_ref[0:1] + before
        slot = jnp.zeros((TM, LANES), I32)
        for k in range(TOP_K):
            sk = jnp.sum(jnp.where(lane == e[k], pos, 0.0), axis=1, keepdims=True)
            slot = jnp.where(lane == k, sk.astype(I32), slot)
        slot_ref[...] = slot
        run_ref[...] += jnp.broadcast_to(colsum, run_ref.shape)


def _route(topi_all, n_blocks):
    n_total = topi_all.shape[0]
    n_blocks_pad = -(-n_blocks // SUBLANES) * SUBLANES
    stat = lambda: pltpu.VMEM((SUBLANES, LANES), F32)
    return pl.pallas_call(
        functools.partial(_route_kernel, n_blocks_pad=n_blocks_pad),
        grid=(2, n_total // TM),
        in_specs=[pl.BlockSpec((TM, LANES), lambda p, i: (i, 0))],
        out_specs=[
            pl.BlockSpec((TM, LANES), lambda p, i: (i * p, 0)),
            pl.BlockSpec((n_blocks_pad, LANES), lambda p, i: (0, 0)),
            pl.BlockSpec((SUBLANES, LANES), lambda p, i: (0, 0)),
        ],
        out_shape=[
            jax.ShapeDtypeStruct((n_total, LANES), I32),
            jax.ShapeDtypeStruct((n_blocks_pad, LANES), I32),
            jax.ShapeDtypeStruct((SUBLANES, LANES), I32),
        ],
        scratch_shapes=[stat(), stat(), stat(), pltpu.VMEM((TM, TM), BF16)],
        compiler_params=pltpu.CompilerParams(dimension_semantics=("arbitrary", "arbitrary")),
        name="route",
    )(topi_all)


TOKEN_BITS = 15


def _invert_kernel(slot_ref, meta_ref, inv_ref, *, n_tokens):
    n_assign = TOP_K * n_tokens

    def pack(row, tok):
        return lax.shift_left(jnp.asarray(row, I32), jnp.int32(TOKEN_BITS)) | jnp.asarray(tok, I32)

    @pl.when(pl.program_id(0) == 0)
    def _():
        def expert(e, seen):
            seen = seen + meta_ref[e]

            def pad(s, carry):
                inv_ref[s] = pack(n_assign + s - seen, 0)
                return carry
            lax.fori_loop(meta_ref[N_EXPERTS + e] + meta_ref[e], meta_ref[2 * N_EXPERTS + e], pad, 0)
            return seen
        lax.fori_loop(0, N_EXPERTS, expert, 0)

        def unused(s, carry):
            inv_ref[s] = pack(s, 0)
            return carry
        lax.fori_loop(meta_ref[3 * N_EXPERTS] * BLK, inv_ref.shape[0], unused, 0)

    tok0 = pl.program_id(0) * TM
    step = (1 << TOKEN_BITS) + 1
    for k in range(TOP_K):
        row0 = k * n_tokens + tok0

        def real(_, carry):
            row, val = carry
            for u in range(SUBLANES):
                inv_ref[slot_ref[row + u]] = val + u * step
            return row + SUBLANES, val + SUBLANES * step
        lax.fori_loop(0, TM // SUBLANES, real, (row0, pack(row0, tok0)))


def _invert(slot_kmajor, meta_flat, n_rows):
    n_tokens = slot_kmajor.shape[0] // TOP_K
    assert n_tokens <= 1 << TOKEN_BITS and n_rows <= 1 << (32 - TOKEN_BITS)
    return pl.pallas_call(
        functools.partial(_invert_kernel, n_tokens=n_tokens),
        grid_spec=pltpu.PrefetchScalarGridSpec(
            num_scalar_prefetch=2, grid=(n_tokens // TM,), in_specs=[],
            out_specs=pl.BlockSpec(memory_space=pltpu.SMEM)),
        out_shape=jax.ShapeDtypeStruct((n_rows,), I32),
        compiler_params=pltpu.CompilerParams(dimension_semantics=("arbitrary",)),
        name="invert",
    )(slot_kmajor, meta_flat)


GLU_GROUP = 2 * LANES


def _row_tile(ref, row):
    return ref.at[pl.ds(pl.multiple_of(row * ROW_TILES, ROW_TILES), ROW_TILES), :]


def _gather_rows(inv_ref, m_hbm, xbuf, sem, blk):
    for r in range(BLK):
        tok = inv_ref[blk * BLK + r] & ((1 << TOKEN_BITS) - 1)
        pltpu.make_async_copy(_row_tile(m_hbm, tok), _row_tile(xbuf, r), sem).start()


def _scatter_rows(inv_ref, ybuf, out_hbm, sem, blk):
    for r in range(BLK):
        row = lax.shift_right_logical(inv_ref[blk * BLK + r], TOKEN_BITS)
        pltpu.make_async_copy(_row_tile(ybuf, r), _row_tile(out_hbm, row), sem).start()


def _moe_kernel(inv_ref, be_ref, meta_ref, m_hbm, wup_hbm, bup_ref, wdn_hbm, bdn_ref, out_hbm,
                xbuf0, xbuf1, ybuf0, ybuf1, wup_st, wdn_st, wup_bf, wdn_bf, perm_ref, zbuf_ref,
                gsem, ssem, wsem, zsem):
    j = pl.program_id(0)
    n_used = meta_ref[3 * N_EXPERTS]
    xbufs, ybufs = (xbuf0, xbuf1), (ybuf0, ybuf1)

    def fetch_weights(e):
        pltpu.make_async_copy(wup_hbm.at[e], wup_st, wsem.at[0]).start(priority=1)
        pltpu.make_async_copy(wdn_hbm.at[e], wdn_st, wsem.at[1]).start(priority=1)

    def wait_gather(par):
        pltpu.make_async_copy(m_hbm.at[pl.ds(0, BLK * ROW_TILES), :], xbufs[par], gsem.at[par]).wait()

    def wait_scatter(par):
        pltpu.make_async_copy(ybufs[par], out_hbm.at[pl.ds(0, BLK * ROW_TILES), :], ssem.at[par]).wait()

    @pl.when(j == 0)
    def _():
        r = lax.broadcasted_iota(I32, (GLU_GROUP, GLU_GROUP), 0)
        c = lax.broadcasted_iota(I32, (GLU_GROUP, GLU_GROUP), 1)
        src = jnp.where(c < LANES, 2 * c, 2 * (c - LANES) + 1)
        perm_ref[...] = jnp.where(r == src, 1.0, 0.0).astype(BF16)
        zbuf_ref[...] = jnp.zeros_like(zbuf_ref)
        fetch_weights(be_ref[0])
        _gather_rows(inv_ref, m_hbm, xbuf0, gsem.at[0], 0)

    for par in range(2):
        b = 2 * j + par
        other = 1 - par
        active = b < n_used
        e = be_ref[b]

        @pl.when(active & (b >= 2))
        def _():
            wait_scatter(par)

        @pl.when(active)
        def _():
            wait_gather(par)
            _gather_rows(inv_ref, m_hbm, xbufs[other], gsem.at[other], b + 1)

        @pl.when(active & (b >= 1))
        def _():
            _scatter_rows(inv_ref, ybufs[other], out_hbm, ssem.at[other], b - 1)

        @pl.when(active & ((b == 0) | (e != be_ref[jnp.maximum(b - 1, 0)])))
        def _():
            pltpu.make_async_copy(wup_hbm.at[0], wup_st, wsem.at[0]).wait()
            pltpu.make_async_copy(wdn_hbm.at[0], wdn_st, wsem.at[1]).wait()
            for g in range(2 * D_FF // GLU_GROUP):
                wg = wup_st[:, g * GLU_GROUP:(g + 1) * GLU_GROUP].astype(BF16)
                pg = _dot(wg, perm_ref[...]).astype(BF16)
                wup_bf[:, g * LANES:(g + 1) * LANES] = pg[:, 0:LANES]
                wup_bf[:, D_FF + g * LANES:D_FF + (g + 1) * LANES] = pg[:, LANES:]
            wdn_bf[...] = wdn_st[...].astype(BF16)
            next_first = _div(meta_ref[2 * N_EXPERTS + e], BLK)

            @pl.when(next_first < n_used)
            def _():
                fetch_weights(be_ref[next_first])

        @pl.when(active)
        def _():
            h = _dot(_load_row_tiles(xbufs[par], BF16), wup_bf[...]) + bup_ref[pl.ds(e, 1), :]
            glu = jnp.minimum(h[:, 0:D_FF], SWIGLU_LIMIT)
            lin = jnp.clip(h[:, D_FF:], -SWIGLU_LIMIT, SWIGLU_LIMIT)
            act = glu * jax.nn.sigmoid(SWIGLU_ALPHA * glu) * (lin + 1.0)
            _store_row_tiles(ybufs[par], _dot(act.astype(BF16), wdn_bf[...]) + bdn_ref[pl.ds(e, 1), :])

        @pl.when(b == n_used)
        def _():
            wait_gather(par)
            _scatter_rows(inv_ref, ybufs[other], out_hbm, ssem.at[other], b - 1)
            wait_scatter(other)

            @pl.when(b >= 2)
            def _():
                wait_scatter(par)

        @pl.when(b >= n_used)
        def _():
            half = zbuf_ref.shape[0]
            for part in range(BLK * ROW_TILES // half):
                start = pl.multiple_of(b * (BLK * ROW_TILES) + part * half, half)
                cp = pltpu.make_async_copy(zbuf_ref, out_hbm.at[pl.ds(start, half), :], zsem)
                cp.start()
                cp.wait()


def _moe(inv, block_e, meta_flat, m_all, wup, bup, wdn, bdn):
    n_rows = inv.shape[0]
    any_spec = pl.BlockSpec(memory_space=pl.ANY)
    xy = lambda: pltpu.VMEM((BLK * ROW_TILES, LANES), F32)
    return pl.pallas_call(
        _moe_kernel,
        grid_spec=pltpu.PrefetchScalarGridSpec(
            num_scalar_prefetch=3,
            grid=(n_rows // (2 * BLK),),
            in_specs=[
                any_spec, any_spec,
                pl.BlockSpec((N_EXPERTS, 2 * D_FF), lambda j, *_: (0, 0)),
                any_spec,
                pl.BlockSpec((N_EXPERTS, D), lambda j, *_: (0, 0)),
            ],
            out_specs=any_spec,
            scratch_shapes=[
                xy(), xy(), xy(), xy(),
                pltpu.VMEM((D, 2 * D_FF), F32), pltpu.VMEM((D_FF, D), F32),
                pltpu.VMEM((D, 2 * D_FF), BF16), pltpu.VMEM((D_FF, D), BF16),
                pltpu.VMEM((GLU_GROUP, GLU_GROUP), BF16),
                pltpu.VMEM((BLK * ROW_TILES // 2, LANES), F32),
                pltpu.SemaphoreType.DMA((2,)), pltpu.SemaphoreType.DMA((2,)),
                pltpu.SemaphoreType.DMA((2,)), pltpu.SemaphoreType.DMA,
            ],
        ),
        out_shape=jax.ShapeDtypeStruct((n_rows * ROW_TILES, LANES), F32),
        compiler_params=pltpu.CompilerParams(
            dimension_semantics=("arbitrary",), vmem_limit_bytes=VMEM_LIMIT),
        name="moe",
    )(inv, block_e, meta_flat, m_all, wup, bup, wdn, bdn)


def _tail_kernel(ye0_ref, ye1_ref, ye2_ref, ye3_ref, gate_ref, h1_ref, pp_ref, pd_ref, gple_ref,
                 wpg_ref, wple_ref, gfin_ref, yp_ref, yd_ref, *, prompt_tiles):
    i = pl.program_id(0)
    gate = gate_ref[...]
    moe = None
    for k, ye_ref in enumerate((ye0_ref, ye1_ref, ye2_ref, ye3_ref)):
        term = gate[:, k:k + 1] * _load_row_tiles(ye_ref)
        moe = term if moe is None else moe + term
    h2 = h1_ref[...] + moe
    pg = jax.nn.sigmoid(_dot(_rms(h2, gple_ref[...]).astype(BF16), wpg_ref[...]))
    p = jnp.where(i < prompt_tiles, pp_ref[...], pd_ref[...])
    h3 = h2 + pg * _dot(p.astype(BF16), wple_ref[...])
    y = _rms(h3, gfin_ref[...])

    @pl.when(i < prompt_tiles)
    def _():
        yp_ref[...] = y

    @pl.when(i >= prompt_tiles)
    def _():
        yd_ref[...] = y


def _tail(y_exp, gate_all, h1_all, p_prompt, p_dec, gple, wpg, wple, gfin):
    n_total = h1_all.shape[0]
    n_prompt = p_prompt.shape[0]
    prompt_tiles = n_prompt // TMT
    tiles = n_total // TMT
    row_spec = lambda w: pl.BlockSpec((TMT, w), lambda i: (i, 0))
    plane = lambda k: pl.BlockSpec((TMT * ROW_TILES, LANES), lambda i: (k * tiles + i, 0))
    prompt_rows = lambda w: pl.BlockSpec((TMT, w), lambda i: (jnp.minimum(i, prompt_tiles - 1), 0))
    dec_rows = lambda w: pl.BlockSpec((TMT, w), lambda i: (jnp.maximum(i - prompt_tiles, 0), 0))
    return pl.pallas_call(
        functools.partial(_tail_kernel, prompt_tiles=prompt_tiles),
        grid=(tiles,),
        in_specs=[
            plane(0), plane(1), plane(2), plane(3), row_spec(LANES), row_spec(D),
            prompt_rows(PLE_DIM), dec_rows(PLE_DIM),
            _const_spec((1, D)), _const_spec((D, D)), _const_spec((PLE_DIM, D)), _const_spec((1, D)),
        ],
        out_specs=[prompt_rows(D), dec_rows(D)],
        out_shape=[jax.ShapeDtypeStruct((n_prompt, D), F32),
                   jax.ShapeDtypeStruct((n_total - n_prompt, D), F32)],
        compiler_params=pltpu.CompilerParams(
            dimension_semantics=("arbitrary",), vmem_limit_bytes=VMEM_LIMIT),
        name="tail",
    )(y_exp, y_exp, y_exp, y_exp, gate_all, h1_all, p_prompt, p_dec, gple, wpg, wple, gfin)


def _widen_heads(w_q):
    wq = w_q.reshape(D, N_HEADS, HEAD_DIM)
    z = jnp.zeros_like(wq)
    lo = jnp.concatenate([wq, z], axis=-1)
    hi = jnp.concatenate([z, wq], axis=-1)
    kv_of_head = (jnp.arange(N_HEADS) // GROUP)[None, :, None]
    return jnp.where(kv_of_head == 0, lo, hi).reshape(D, N_HEADS * LANES)


def kernel(x_prompt, x_sample, state_conv, cache_k, cache_v, p_prompt, p_sample, g_mix, w_in, conv_w,
           sinks, w_branch, w_out, g_ffn, w_router, b_router, w_up, b_up, w_down, b_down, g_ple,
           w_ple_gate, w_ple, g_final):
    depth = w_in.shape[0]
    assert depth == 1, "single-layer step"
    n_batch, seq, _ = x_prompt.shape
    n_seq, n_new, _ = x_sample.shape
    wb = cache_k.shape[2]
    assert wb == PAST_WB == WINDOW and seq % TM == 0 and n_seq * n_new == TM
    n_prompt = n_batch * seq
    n_dec = n_seq * n_new
    n_total = n_prompt + n_dec
    n_blocks = -(-(n_total * TOP_K) // BLK) + N_EXPERTS
    assert n_blocks % 2 == 0
    n_rows = n_blocks * BLK

    win = w_in[0].astype(BF16)
    w_q = w_in[0][:, C_Q:C_KV]
    win_s = jnp.concatenate([w_in[0][:, :C_Q], _widen_heads(w_q), w_in[0][:, C_KV:]], axis=1).astype(BF16)
    wbr = w_branch[0].astype(BF16)
    wbr1e = jnp.transpose(_widen_heads(jnp.transpose(w_branch[0, 1]))).astype(BF16)
    wout = w_out[0].astype(BF16)
    wr_f = jnp.pad(w_router[0], ((0, 0), (0, LANES - N_EXPERTS)))
    wr_hi = wr_f.astype(BF16)
    wr = jnp.concatenate([wr_hi, (wr_f - wr_hi.astype(F32)).astype(BF16)], axis=1)
    br = jnp.pad(b_router[0], (0, LANES - N_EXPERTS), constant_values=-1e30).reshape(1, LANES)
    gmix = g_mix[0].reshape(1, D)
    gffn = g_ffn[0].reshape(1, D)
    bu = b_up[0].reshape(N_EXPERTS, D_FF, 2)
    bup = jnp.concatenate([bu[..., 0], bu[..., 1]], axis=-1)
    bdn = b_down[0]

    xp = x_prompt.reshape(n_prompt, D)
    h1_all, m_all, topi_all, gate_all, conv_p, k_p, v_p = _mixer_prompt(
        xp, gmix, win, conv_w[0], sinks[0], wbr, wout, gffn, wr, br, n_total, seq)
    xs_t = jnp.transpose(x_sample, (1, 0, 2)).reshape(n_dec, D)
    state_t = jnp.transpose(state_conv[0], (1, 0, 2))
    ck = cache_k[0].reshape(n_seq * wb, KV_W)
    cv = cache_v[0].reshape(n_seq * wb, KV_W)
    h1_all, m_all, topi_all, gate_all, conv_s, k_s, v_s = _mixer_sample(
        xs_t, gmix, win_s, conv_w[0], state_t, sinks[0], ck, cv, wbr[0], wbr1e, wout, gffn, wr, br,
        h1_all, m_all, topi_all, gate_all, n_seq, n_new)

    slot_w, tab, meta = _route(topi_all, n_blocks)
    slot_kmajor = jnp.transpose(slot_w[:, :TOP_K]).reshape(-1)
    meta_flat = jnp.concatenate([meta[0:3, :N_EXPERTS].reshape(-1), meta[3, 0:1]])
    inv = _invert(slot_kmajor, meta_flat, n_rows)
    y_exp = _moe(inv, tab[:n_blocks, 0], meta_flat, m_all, w_up[0], bup, w_down[0], bdn)

    y_p, y_d = _tail(y_exp, gate_all, h1_all, p_prompt[0].reshape(n_prompt, PLE_DIM),
                     jnp.transpose(p_sample[0], (1, 0, 2)).reshape(n_dec, PLE_DIM),
                     g_ple[0].reshape(1, D), w_ple_gate[0].astype(BF16), w_ple[0].astype(BF16),
                     g_final.reshape(1, D))

    y_prompt = y_p.reshape(n_batch, seq, D)
    y_sample = jnp.transpose(y_d.reshape(n_new, n_seq, D), (1, 0, 2))
    conv_prompt = conv_p[:, SUBLANES - 2:, :][None]
    conv_sample = jnp.transpose(conv_s, (1, 0, 2))[None]
    kv_shape = (1, -1, wb, N_KV, HEAD_DIM)
    return (y_prompt, y_sample, conv_prompt, conv_sample,
            k_p.reshape(kv_shape), v_p.reshape(kv_shape), k_s.reshape(kv_shape), v_s.reshape(kv_shape))
```
